```python
import jax, jax.numpy as jnp
from jax import lax
import numpy as np

D_MODEL = 1024
BATCH = 2
SEQ = 8192
DEPTH = 4
DEC_BATCH = 1
DEC_SEQ = 16384
PAST_LEN = 128

N_MIXERS = 3
N_LAYERS_A = (DEPTH + 2) // 3
N_LAYERS_B = (DEPTH + 1) // 3
N_LAYERS_C = DEPTH // 3
CHUNK = 128
A_WIDTH = D_MODEL
A_GROUPS = 8
A_HEAD = A_WIDTH // A_GROUPS
B_HEADS = 16
B_KV_HEADS = 4
B_HEAD_DIM = 64
B_Q_PER_KV = B_HEADS // B_KV_HEADS
WINDOW = 128
BLOCK = 128
C_WIDTH = D_MODEL
C_GROUPS = 8
C_GROUP_DIM = C_WIDTH // C_GROUPS
FF_DIM = 2816
CONV_WIDTH = 3
EPS = 1e-6
NEG_INF = -1e30

kernel_name = "hybrid_bidir_gmlp_swa_fnet_encoder"


def rmsnorm(x, g):
    xf = x.astype(jnp.float32)
    y = xf * lax.rsqrt(jnp.mean(xf * xf, axis=-1, keepdims=True) + EPS)
    return y.astype(x.dtype) * g


def alibi_slopes():
    return jnp.exp2(-8.0 * jnp.arange(1, B_HEADS + 1, dtype=jnp.float32) / B_HEADS)


def gmlp_chunk_mixer(h, w_in, g_v, w_s, b_s, w_out):
    b, s, _ = h.shape
    uv = jax.nn.gelu(h @ w_in, approximate=False)
    u, v = jnp.split(uv, 2, axis=-1)
    v = rmsnorm(v, g_v).reshape(b, s // CHUNK, CHUNK, A_GROUPS, A_HEAD)
    sv = jnp.einsum("gts,bnsgd->bntgd", w_s, v) + b_s.T[None, None, :, :, None]
    return (u * sv.reshape(b, s, A_WIDTH)) @ w_out


def windowed_gqa_mixer(h, w_qkv, sinks, w_o):
    b, s, _ = h.shape
    nb = s // BLOCK
    qkv = h @ w_qkv
    q, k, v = jnp.split(qkv, [B_HEADS * B_HEAD_DIM, (B_HEADS + B_KV_HEADS) * B_HEAD_DIM], axis=-1)
    q = q.reshape(b, nb, BLOCK, B_KV_HEADS, B_Q_PER_KV, B_HEAD_DIM)

    def band(t):
        t = t.reshape(b, s, B_KV_HEADS, B_HEAD_DIM)
        tp = jnp.pad(t, ((0, 0), (BLOCK, BLOCK), (0, 0), (0, 0)))
        tp = tp.reshape(b, nb + 2, BLOCK, B_KV_HEADS, B_HEAD_DIM)
        return jnp.concatenate([tp[:, :-2], tp[:, 1:-1], tp[:, 2:]], axis=2)

    kb, vb = band(k), band(v)
    scores = jnp.einsum("bnqkgd,bnskd->bnkgqs", q, kb,
                        preferred_element_type=jnp.float32) * (B_HEAD_DIM ** -0.5)
    qpos = jnp.arange(nb)[:, None, None] * BLOCK + jnp.arange(BLOCK)[None, :, None]
    kpos = (jnp.arange(nb)[:, None, None] - 1) * BLOCK + jnp.arange(3 * BLOCK)[None, None, :]
    dist = jnp.abs(qpos - kpos)
    valid = (dist <= WINDOW) & (kpos >= 0) & (kpos < s)
    slopes = alibi_slopes().reshape(B_KV_HEADS, B_Q_PER_KV)
    alibi = -slopes[None, :, :, None, None] * dist.astype(jnp.float32)[:, None, None]
    scores = jnp.where(valid[None, :, None, None], scores + alibi[None], NEG_INF)
    sink = jnp.broadcast_to(
        sinks.astype(jnp.float32).reshape(B_KV_HEADS, B_Q_PER_KV)[None, None, :, :, None, None],
        scores.shape[:-1] + (1,))
    p = jax.nn.softmax(jnp.concatenate([scores, sink], axis=-1), axis=-1)[..., :-1]
    o = jnp.einsum("bnkgqs,bnskd->bnqkgd", p.astype(vb.dtype), vb)
    return o.reshape(b, s, B_HEADS * B_HEAD_DIM) @ w_o


def fourier_mixer(h, w_in, w_out):
    b, s, _ = h.shape
    z = (h @ w_in).reshape(b, s, C_GROUPS, C_GROUP_DIM).astype(jnp.float32)
    f = jnp.real(jnp.fft.fft2(z, axes=(1, 3), norm="ortho")).astype(h.dtype)
    return f.reshape(b, s, C_WIDTH) @ w_out


def conv_gated_ffn(h, w_up, w_conv, b_conv, w_down):
    s = h.shape[1]
    up = h @ w_up
    p = jnp.pad(up, ((0, 0), (1, 1), (0, 0)))
    up = p[:, :s] * w_conv[0] + p[:, 1:s + 1] * w_conv[1] + p[:, 2:] * w_conv[2] + b_conv
    a, g = jnp.split(up, 2, axis=-1)
    return (a * jax.nn.silu(g)) @ w_down


def trunk(x, c, w_ada, b_ada, norm_g, a_w_in, a_g_v, a_w_s, a_b_s, a_w_out,
          b_w_qkv, b_sinks, b_w_o, c_w_in, c_w_out,
          f_w_up, f_w_conv, f_b_conv, f_w_down, g_final):
    cs = jax.nn.silu(c)
    for i in range(DEPTH):
        mod = cs @ w_ada[i] + b_ada[i]
        sh1, sc1, gt1, sh2, sc2, gt2 = jnp.split(mod[:, None, :], 6, axis=-1)
        h = rmsnorm(x, norm_g[i, 0]) * (1 + sc1) + sh1
        kind, j = i % N_MIXERS, i // N_MIXERS
        if kind == 0:
            m = gmlp_chunk_mixer(h, a_w_in[j], a_g_v[j], a_w_s[j], a_b_s[j], a_w_out[j])
        elif kind == 1:
            m = windowed_gqa_mixer(h, b_w_qkv[j], b_sinks[j], b_w_o[j])
        else:
            m = fourier_mixer(h, c_w_in[j], c_w_out[j])
        x = x + gt1 * m
        h = rmsnorm(x, norm_g[i, 1]) * (1 + sc2) + sh2
        x = x + gt2 * conv_gated_ffn(h, f_w_up[i], f_w_conv[i], f_b_conv[i], f_w_down[i])
    return rmsnorm(x, g_final)


def setup_inputs(seed: int = 0) -> dict:
    key = jax.random.key(seed)
    ks = jax.random.split(key, 24)
    D = D_MODEL
    f32 = jnp.float32

    def nrm(k, shape, scale):
        return jax.random.normal(k, shape, f32) * scale

    qkv_w = (B_HEADS + 2 * B_KV_HEADS) * B_HEAD_DIM
    conv_center = jnp.zeros((CONV_WIDTH, 1), f32).at[1].set(1.0)
    return {
        "x_prompt": nrm(ks[0], (BATCH, SEQ, D), 1.0),
        "x_sample": nrm(ks[1], (DEC_BATCH, DEC_SEQ, D), 1.0),
        "c_prompt": nrm(ks[2], (BATCH, D), 1.0),
        "c_sample": nrm(ks[3], (DEC_BATCH, D), 1.0),
        "w_ada": nrm(ks[4], (DEPTH, D, 6 * D), 0.5 * D ** -0.5),
        "b_ada": nrm(ks[5], (DEPTH, 6 * D), 0.01),
        "norm_g": 1.0 + nrm(ks[6], (DEPTH, 2, D), 0.05),
        "a_w_in": nrm(ks[7], (N_LAYERS_A, D, 2 * A_WIDTH), D ** -0.5),
        "a_g_v": 1.0 + nrm(ks[8], (N_LAYERS_A, A_WIDTH), 0.05),
        "a_w_s": nrm(ks[9], (N_LAYERS_A, A_GROUPS, CHUNK, CHUNK), CHUNK ** -0.5),
        "a_b_s": 1.0 + nrm(ks[10], (N_LAYERS_A, A_GROUPS, CHUNK), 0.1),
        "a_w_out": nrm(ks[11], (N_LAYERS_A, A_WIDTH, D), A_WIDTH ** -0.5),
        "b_w_qkv": nrm(ks[12], (N_LAYERS_B, D, qkv_w), D ** -0.5),
        "b_sinks": nrm(ks[13], (N_LAYERS_B, B_HEADS), 0.5),
        "b_w_o": nrm(ks[14], (N_LAYERS_B, B_HEADS * B_HEAD_DIM, D), (B_HEADS * B_HEAD_DIM) ** -0.5),
        "c_w_in": nrm(ks[15], (N_LAYERS_C, D, C_WIDTH), D ** -0.5),
        "c_w_out": nrm(ks[16], (N_LAYERS_C, C_WIDTH, D), C_WIDTH ** -0.5),
        "f_w_up": nrm(ks[17], (DEPTH, D, 2 * FF_DIM), D ** -0.5),
        "f_w_conv": conv_center[None] + nrm(ks[18], (DEPTH, CONV_WIDTH, 2 * FF_DIM), 0.2),
        "f_b_conv": nrm(ks[19], (DEPTH, 2 * FF_DIM), 0.01),
        "f_w_down": nrm(ks[20], (DEPTH, FF_DIM, D), FF_DIM ** -0.5),
        "g_final": 1.0 + nrm(ks[21], (D,), 0.05),
    }


def reference(x_prompt, x_sample, c_prompt, c_sample, w_ada, b_ada, norm_g,
              a_w_in, a_g_v, a_w_s, a_b_s, a_w_out, b_w_qkv, b_sinks, b_w_o,
              c_w_in, c_w_out, f_w_up, f_w_conv, f_b_conv, f_w_down, g_final):
    y_prompt = trunk(x_prompt, c_prompt, w_ada, b_ada, norm_g, a_w_in, a_g_v, a_w_s, a_b_s,
                     a_w_out, b_w_qkv, b_sinks, b_w_o, c_w_in, c_w_out,
                     f_w_up, f_w_conv, f_b_conv, f_w_down, g_final)
    y_sample = trunk(x_sample, c_sample, w_ada, b_ada, norm_g, a_w_in, a_g_v, a_w_s, a_b_s,
                     a_w_out, b_w_qkv, b_sinks, b_w_o, c_w_in, c_w_out,
                     f_w_up, f_w_conv, f_b_conv, f_w_down, g_final)
    return (y_prompt, y_sample)
```

```python
import functools

import numpy as np
import jax
import jax.numpy as jnp
from jax import lax
from jax.experimental import pallas as pl
from jax.experimental.pallas import tpu as pltpu

F32 = jnp.float32
BF16 = jnp.bfloat16

D_MODEL = 1024
DEPTH = 4
N_MIXERS = 3
EPS = 1e-6
NEG_INF = -1e30
CHUNK = 128
A_GROUPS = 8
A_HEAD = D_MODEL // A_GROUPS
B_HEADS = 16
B_KV_HEADS = 4
B_HEAD_DIM = 64
B_Q_PER_KV = B_HEADS // B_KV_HEADS
WINDOW = 128
BLOCK = 128
Q_WIDTH = B_HEADS * B_HEAD_DIM
KV_WIDTH = B_KV_HEADS * B_HEAD_DIM
C_GROUPS = 8
C_GROUP_DIM = D_MODEL // C_GROUPS
FF_DIM = 2816
FF_CHUNK = 256
HALO = 8

MOD_ROWS = 8
MOD_NT = 1536
VMEM_LIMIT = 56 * 1024 * 1024

TOKEN_TILE = 512
FOURIER_TOKENS = 1024


def _params(*sem):
    return pltpu.CompilerParams(dimension_semantics=sem, vmem_limit_bytes=VMEM_LIMIT)


def _norm_mod(x, gain, scale, shift):
    ms = jnp.mean(x * x, axis=-1, keepdims=True)
    return x * lax.rsqrt(ms + EPS) * (gain * (1.0 + scale)) + shift


def _mod_kernel(c_ref, w_ref, b_ref, o_ref):
    c = c_ref[...]
    cs = c * jax.nn.sigmoid(c)
    o_ref[0] = jnp.dot(cs, w_ref[0], preferred_element_type=F32) + b_ref[0]


def _modulation(c_all, w_ada, b_ada):
    n_out = w_ada.shape[-1]
    return pl.pallas_call(
        _mod_kernel,
        grid=(DEPTH, n_out // MOD_NT),
        in_specs=[
            pl.BlockSpec((MOD_ROWS, D_MODEL), lambda i, j: (0, 0)),
            pl.BlockSpec((1, D_MODEL, MOD_NT), lambda i, j: (i, 0, j)),
            pl.BlockSpec((1, 1, MOD_NT), lambda i, j: (i, 0, j)),
        ],
        out_specs=pl.BlockSpec((1, MOD_ROWS, MOD_NT), lambda i, j: (i, 0, j)),
        out_shape=jax.ShapeDtypeStruct((DEPTH, MOD_ROWS, n_out), F32),
        compiler_params=_params("parallel", "parallel"),
        name="modulation",
    )(c_all, w_ada, b_ada.reshape(DEPTH, 1, n_out))


def _gmlp_kernel(x_ref, mod_ref, g_ref, win_ref, gv_ref, ws_ref, bs_ref, wout_ref, o_ref, gated_ref, *, tile):
    x = x_ref[0]
    mod = mod_ref[0]
    h = _norm_mod(x, g_ref[...], mod[1:2], mod[0:1]).astype(BF16)
    uv = jnp.dot(h, win_ref[...], preferred_element_type=F32)
    uv = 0.5 * uv * (1.0 + lax.erf(uv * np.float32(1.0 / np.sqrt(2.0))))
    u = uv[:, :D_MODEL]
    v = uv[:, D_MODEL:]
    v = v * lax.rsqrt(jnp.mean(v * v, axis=-1, keepdims=True) + EPS) * gv_ref[...]
    vb = v.astype(BF16)
    for c in range(tile // CHUNK):
        rows = slice(c * CHUNK, (c + 1) * CHUNK)
        for g in range(A_GROUPS):
            cols = slice(g * A_HEAD, (g + 1) * A_HEAD)
            sv = jnp.dot(ws_ref[g], vb[rows, cols], preferred_element_type=F32) + bs_ref[g]
            gated_ref[rows, cols] = (u[rows, cols] * sv).astype(BF16)
    m = jnp.dot(gated_ref[...], wout_ref[...], preferred_element_type=F32)
    o_ref[0] = x + mod[2:3] * m


def _gmlp_layer(x, mod, gain, w_in, g_v, w_s, b_s, w_out):
    b, s, d = x.shape
    tile = min(TOKEN_TILE, s)
    bs_full = jnp.broadcast_to(b_s[:, :, None], (A_GROUPS, CHUNK, A_HEAD))
    const2 = lambda bi, i: (0, 0)
    const3 = lambda bi, i: (0, 0, 0)
    return pl.pallas_call(
        functools.partial(_gmlp_kernel, tile=tile),
        grid=(b, s // tile),
        in_specs=[
            pl.BlockSpec((1, tile, d), lambda bi, i: (bi, i, 0)),
            pl.BlockSpec((1, 6, d), lambda bi, i: (bi, 0, 0)),
            pl.BlockSpec((1, d), const2),
            pl.BlockSpec((d, 2 * d), const2),
            pl.BlockSpec((1, d), const2),
            pl.BlockSpec((A_GROUPS, CHUNK, CHUNK), const3),
            pl.BlockSpec((A_GROUPS, CHUNK, A_HEAD), const3),
            pl.BlockSpec((d, d), const2),
        ],
        out_specs=pl.BlockSpec((1, tile, d), lambda bi, i: (bi, i, 0)),
        out_shape=jax.ShapeDtypeStruct(x.shape, F32),
        scratch_shapes=[pltpu.VMEM((tile, d), BF16)],
        compiler_params=_params("parallel", "parallel"),
        name="gmlp_mixer",
    )(x, mod, gain.reshape(1, d), w_in.astype(BF16), g_v.reshape(1, d), w_s.astype(BF16), bs_full,
      w_out.astype(BF16))


def _qkv_kernel(x_ref, mod_ref, g_ref, w_ref, q_ref, k_ref, v_ref):
    mod = mod_ref[0]
    h = _norm_mod(x_ref[0], g_ref[...], mod[1:2], mod[0:1]).astype(BF16)
    qkv = jnp.dot(h, w_ref[...], preferred_element_type=F32)
    q_ref[0] = (qkv[:, :Q_WIDTH] * (B_HEAD_DIM ** -0.5)).astype(BF16)
    k_ref[0] = qkv[:, Q_WIDTH:Q_WIDTH + KV_WIDTH].astype(BF16)
    v_ref[0] = qkv[:, Q_WIDTH + KV_WIDTH:].astype(BF16)


def _qkv_proj(x, mod, gain, w_qkv):
    b, s, d = x.shape
    tile = min(TOKEN_TILE, s)
    n = w_qkv.shape[-1]
    tok = lambda bi, i: (bi, i, 0)
    const2 = lambda bi, i: (0, 0)
    return pl.pallas_call(
        _qkv_kernel,
        grid=(b, s // tile),
        in_specs=[
            pl.BlockSpec((1, tile, d), tok),
            pl.BlockSpec((1, 6, d), lambda bi, i: (bi, 0, 0)),
            pl.BlockSpec((1, d), const2),
            pl.BlockSpec((d, n), const2),
        ],
        out_specs=[
            pl.BlockSpec((1, tile, Q_WIDTH), tok),
            pl.BlockSpec((1, tile, KV_WIDTH), tok),
            pl.BlockSpec((1, tile, KV_WIDTH), tok),
        ],
        out_shape=[
            jax.ShapeDtypeStruct((b, s, Q_WIDTH), BF16),
            jax.ShapeDtypeStruct((b, s, KV_WIDTH), BF16),
            jax.ShapeDtypeStruct((b, s, KV_WIDTH), BF16),
        ],
        compiler_params=_params("parallel", "parallel"),
        name="qkv_proj",
    )(x, mod, gain.reshape(1, d), w_qkv.astype(BF16))


def _attn_kernel(sink_ref, x_ref, mod_ref, q_ref, kp_ref, k_ref, kn_ref, vp_ref, v_ref, vn_ref, wo_ref,
                 o_ref, kext_ref, vext_ref, ocat_ref, *, tile, seq):
    i = pl.program_id(1)
    kext_ref[0:BLOCK] = kp_ref[0]
    kext_ref[BLOCK:BLOCK + tile] = k_ref[0]
    kext_ref[BLOCK + tile:] = kn_ref[0]
    vext_ref[0:BLOCK] = vp_ref[0]
    vext_ref[BLOCK:BLOCK + tile] = v_ref[0]
    vext_ref[BLOCK + tile:] = vn_ref[0]

    qq = lax.broadcasted_iota(jnp.int32, (BLOCK, 3 * BLOCK), 0)
    kk = lax.broadcasted_iota(jnp.int32, (BLOCK, 3 * BLOCK), 1)
    dist = jnp.abs(qq + BLOCK - kk)
    distf = dist.astype(F32)

    def q_block(qb, carry):
        r0 = pl.multiple_of(qb * BLOCK, BLOCK)
        kpos = (i * (tile // BLOCK) + qb - 1) * BLOCK + kk
        valid = (dist <= WINDOW) & (kpos >= 0) & (kpos < seq)
        for kv in range(B_KV_HEADS):
            kband = kext_ref[pl.ds(r0, 3 * BLOCK), kv * B_HEAD_DIM:(kv + 1) * B_HEAD_DIM]
            vband = vext_ref[pl.ds(r0, 3 * BLOCK), kv * B_HEAD_DIM:(kv + 1) * B_HEAD_DIM]
            for g in range(B_Q_PER_KV):
                hd = kv * B_Q_PER_KV + g
                slope = np.float32(2.0 ** (-8.0 * (hd + 1) / B_HEADS))
                qh = q_ref[0, pl.ds(r0, BLOCK), hd * B_HEAD_DIM:(hd + 1) * B_HEAD_DIM]
                sc = lax.dot_general(qh, kband, (((1,), (1,)), ((), ())), preferred_element_type=F32)
                sc = jnp.where(valid, sc - slope * distf, NEG_INF)
                sink = sink_ref[hd]
                mx = jnp.maximum(jnp.max(sc, axis=-1, keepdims=True), sink)
                e = jnp.exp(sc - mx)
                denom = jnp.sum(e, axis=-1, keepdims=True) + jnp.exp(sink - mx)
                oh = jnp.dot(e.astype(BF16), vband, preferred_element_type=F32)
                ocat_ref[pl.ds(r0, BLOCK), hd * B_HEAD_DIM:(hd + 1) * B_HEAD_DIM] = oh / denom
        return carry

    lax.fori_loop(0, tile // BLOCK, q_block, 0)
    m = jnp.dot(ocat_ref[...].astype(BF16), wo_ref[...], preferred_element_type=F32)
    o_ref[0] = x_ref[0] + mod_ref[0][2:3] * m


def _attn_layer(x, mod, q, k, v, sinks, w_o):
    b, s, d = x.shape
    tile = min(TOKEN_TILE, s)
    bpt = tile // BLOCK
    nblk = s // BLOCK
    tok = lambda bi, i: (bi, i, 0)
    prev = lambda bi, i: (bi, jnp.maximum(i * bpt - 1, 0), 0)
    nxt = lambda bi, i: (bi, jnp.minimum((i + 1) * bpt, nblk - 1), 0)
    kv_main = pl.BlockSpec((1, tile, KV_WIDTH), tok)
    kv_prev = pl.BlockSpec((1, BLOCK, KV_WIDTH), prev)
    kv_next = pl.BlockSpec((1, BLOCK, KV_WIDTH), nxt)
    return pl.pallas_call(
        functools.partial(_attn_kernel, tile=tile, seq=s),
        grid=(b, s // tile),
        in_specs=[
            pl.BlockSpec(memory_space=pltpu.SMEM),
            pl.BlockSpec((1, tile, d), tok),
            pl.BlockSpec((1, 6, d), lambda bi, i: (bi, 0, 0)),
            pl.BlockSpec((1, tile, Q_WIDTH), tok),
            kv_prev, kv_main, kv_next,
            kv_prev, kv_main, kv_next,
            pl.BlockSpec((Q_WIDTH, d), lambda bi, i: (0, 0)),
        ],
        out_specs=pl.BlockSpec((1, tile, d), tok),
        out_shape=jax.ShapeDtypeStruct(x.shape, F32),
        scratch_shapes=[
            pltpu.VMEM((tile + 2 * BLOCK, KV_WIDTH), BF16),
            pltpu.VMEM((tile + 2 * BLOCK, KV_WIDTH), BF16),
            pltpu.VMEM((tile, Q_WIDTH), F32),
        ],
        compiler_params=_params("parallel", "parallel"),
        name="window_attention",
    )(sinks, x, mod, q, k, k, k, v, v, v, w_o.astype(BF16))


def _dft_tables(seq):
    n1 = BLOCK
    n2 = seq // n1
    k1 = np.arange(n1, dtype=np.int64)
    s1 = np.arange(n1, dtype=np.int64)
    s2 = np.arange(n2, dtype=np.int64)
    pos = s1[None, None, :] * n2 + s2[:, None, None]
    ang = 2.0 * np.pi * ((k1[None, :, None] * pos) % seq).astype(np.float64) / seq
    stage1 = np.concatenate([np.cos(ang), -np.sin(ang)], axis=1).astype(np.float32)
    k2 = np.arange(n2, dtype=np.int64)
    ang2 = 2.0 * np.pi * ((k2[:, None] * s2[None, :]) % n2).astype(np.float64) / n2
    c2, sn2 = np.cos(ang2), np.sin(ang2)
    stage2 = np.block([[c2, sn2], [-sn2, c2]]).astype(np.float32)
    c = np.arange(C_GROUP_DIM, dtype=np.int64)
    angc = 2.0 * np.pi * ((c[:, None] * c[None, :]) % C_GROUP_DIM).astype(np.float64) / C_GROUP_DIM
    chan = (np.concatenate([np.cos(angc), np.sin(angc)], axis=0) / np.sqrt(float(seq) * C_GROUP_DIM)).astype(np.float32)
    return n1, n2, stage1, stage2, chan


def _fourier1_kernel(x_ref, mod_ref, g_ref, win_ref, dft_ref, mid_ref, h_ref, *, n1, grp):
    mod = mod_ref[0]
    gain = g_ref[...] * (1.0 + mod[1:2])
    for g in range(grp):
        xg = x_ref[0, :, g * D_MODEL:(g + 1) * D_MODEL]
        ms = jnp.mean(xg * xg, axis=-1, keepdims=True)
        h_ref[g * n1:(g + 1) * n1] = (xg * lax.rsqrt(ms + EPS) * gain + mod[0:1]).astype(BF16)
    z = jnp.dot(h_ref[...], win_ref[...], preferred_element_type=F32).astype(BF16)
    for g in range(grp):
        y = jnp.dot(dft_ref[g], z[g * n1:(g + 1) * n1], preferred_element_type=F32)
        mid_ref[0, 0, g] = y[:n1].astype(BF16)
        mid_ref[0, 1, g] = y[n1:].astype(BF16)


def _fourier2_kernel(x_ref, mod_ref, mid_ref, dft_ref, chan_ref, wout_ref, o_ref, f_ref, *, n2, grp):
    spec = jnp.dot(dft_ref[...], mid_ref[0], preferred_element_type=F32).astype(BF16)
    for kl in range(grp):
        for cg in range(C_GROUPS):
            lo = kl * D_MODEL + cg * C_GROUP_DIM
            lhs = jnp.concatenate([spec[:n2, lo:lo + C_GROUP_DIM], spec[n2:, lo:lo + C_GROUP_DIM]], axis=1)
            f = jnp.dot(lhs, chan_ref[...], preferred_element_type=F32)
            f_ref[kl * n2:(kl + 1) * n2, cg * C_GROUP_DIM:(cg + 1) * C_GROUP_DIM] = f.astype(BF16)
    m = jnp.dot(f_ref[...], wout_ref[...], preferred_element_type=F32)
    gate = mod_ref[0][2:3]
    for kl in range(grp):
        cols = slice(kl * D_MODEL, (kl + 1) * D_MODEL)
        o_ref[0, :, cols] = x_ref[0, :, cols] + gate * m[kl * n2:(kl + 1) * n2]


def _fourier_layer(x, mod, gain, w_in, w_out):
    b, s, d = x.shape
    n1, n2, stage1, stage2, chan = _dft_tables(s)
    g1 = max(1, min(n2, FOURIER_TOKENS // n1))
    g2 = max(1, min(16, FOURIER_TOKENS // n2))
    const2 = lambda bi, j: (0, 0)
    mid = pl.pallas_call(
        functools.partial(_fourier1_kernel, n1=n1, grp=g1),
        grid=(b, n2 // g1),
        in_specs=[
            pl.BlockSpec((1, n1, g1 * d), lambda bi, j: (bi, 0, j)),
            pl.BlockSpec((1, 6, d), lambda bi, j: (bi, 0, 0)),
            pl.BlockSpec((1, d), const2),
            pl.BlockSpec((d, d), const2),
            pl.BlockSpec((g1, 2 * n1, n1), lambda bi, j: (j, 0, 0)),
        ],
        out_specs=pl.BlockSpec((1, 2, g1, n1, d), lambda bi, j: (bi, 0, j, 0, 0)),
        out_shape=jax.ShapeDtypeStruct((b, 2, n2, n1, d), BF16),
        scratch_shapes=[pltpu.VMEM((g1 * n1, d), BF16)],
        compiler_params=_params("parallel", "parallel"),
        name="fourier_stage1",
    )(x.reshape(b, n1, n2 * d), mod, gain.reshape(1, d), w_in.astype(BF16), jnp.asarray(stage1).astype(BF16))
    out = pl.pallas_call(
        functools.partial(_fourier2_kernel, n2=n2, grp=g2),
        grid=(b, n1 // g2),
        in_specs=[
            pl.BlockSpec((1, n2, g2 * d), lambda bi, j: (bi, 0, j)),
            pl.BlockSpec((1, 6, d), lambda bi, j: (bi, 0, 0)),
            pl.BlockSpec((1, 2 * n2, g2 * d), lambda bi, j: (bi, 0, j)),
            pl.BlockSpec((2 * n2, 2 * n2), const2),
            pl.BlockSpec((2 * C_GROUP_DIM, C_GROUP_DIM), const2),
            pl.BlockSpec((d, d), const2),
        ],
        out_specs=pl.BlockSpec((1, n2, g2 * d), lambda bi, j: (bi, 0, j)),
        out_shape=jax.ShapeDtypeStruct((b, n2, n1 * d), F32),
        scratch_shapes=[pltpu.VMEM((g2 * n2, d), BF16)],
        compiler_params=_params("parallel", "parallel"),
        name="fourier_stage2",
    )(x.reshape(b, n2, n1 * d), mod, mid.reshape(b, 2 * n2, n1 * d), jnp.asarray(stage2).astype(BF16),
      jnp.asarray(chan).astype(BF16), w_out.astype(BF16))
    return out.reshape(b, s, d)


def _ffn_kernel(xp_ref, x_ref, xn_ref, mod_ref, g_ref, wup_ref, wc_ref, bc_ref, wdn_ref, gf_ref, o_ref,
                h_ref, acc_ref, *, tile, final):
    i = pl.program_id(1)
    last = pl.num_programs(1) - 1
    mod = mod_ref[0]
    gain = g_ref[...]
    x = x_ref[0]
    h_ref[0:tile] = _norm_mod(x, gain, mod[4:5], mod[3:4]).astype(BF16)
    h_next = _norm_mod(xn_ref[0], gain, mod[4:5], mod[3:4]) * (i < last).astype(F32)
    h_prev = _norm_mod(xp_ref[0], gain, mod[4:5], mod[3:4]) * (i > 0).astype(F32)
    h_ref[tile:] = jnp.concatenate([h_next, h_prev], axis=0).astype(BF16)

    ext = tile + 2 * HALO
    h = h_ref[...]

    def conv(up, col0):
        cols = slice(col0, col0 + FF_CHUNK)
        w = wc_ref[:, cols]
        before = pltpu.roll(up, 1, 0)[:tile]
        after = pltpu.roll(up, ext - 1, 0)[:tile]
        return before * w[0:1] + up[:tile] * w[1:2] + after * w[2:3] + bc_ref[:, cols]

    for j in range(FF_DIM // FF_CHUNK):
        ca = j * FF_CHUNK
        cg = FF_DIM + j * FF_CHUNK
        a = conv(jnp.dot(h, wup_ref[:, ca:ca + FF_CHUNK], preferred_element_type=F32), ca)
        g = conv(jnp.dot(h, wup_ref[:, cg:cg + FF_CHUNK], preferred_element_type=F32), cg)
        act = (a * (g * jax.nn.sigmoid(g))).astype(BF16)
        part = jnp.dot(act, wdn_ref[ca:ca + FF_CHUNK, :], preferred_element_type=F32)
        if j == 0:
            acc_ref[...] = part
        else:
            acc_ref[...] += part
    y = x + mod[5:6] * acc_ref[...]
    if final:
        y = y * lax.rsqrt(jnp.mean(y * y, axis=-1, keepdims=True) + EPS) * gf_ref[...]
    o_ref[0] = y


def _ffn_layer(x, mod, gain, w_up, w_conv, b_conv, w_down, g_final, final):
    b, s, d = x.shape
    tile = min(TOKEN_TILE, s)
    hpt = tile // HALO
    nh = s // HALO
    tok = lambda bi, i: (bi, i, 0)
    const2 = lambda bi, i: (0, 0)
    return pl.pallas_call(
        functools.partial(_ffn_kernel, tile=tile, final=final),
        grid=(b, s // tile),
        in_specs=[
            pl.BlockSpec((1, HALO, d), lambda bi, i: (bi, jnp.maximum(i * hpt - 1, 0), 0)),
            pl.BlockSpec((1, tile, d), tok),
            pl.BlockSpec((1, HALO, d), lambda bi, i: (bi, jnp.minimum((i + 1) * hpt, nh - 1), 0)),
            pl.BlockSpec((1, 6, d), lambda bi, i: (bi, 0, 0)),
            pl.BlockSpec((1, d), const2),
            pl.BlockSpec((d, 2 * FF_DIM), const2),
            pl.BlockSpec((3, 2 * FF_DIM), const2),
            pl.BlockSpec((1, 2 * FF_DIM), const2),
            pl.BlockSpec((FF_DIM, d), const2),
            pl.BlockSpec((1, d), const2),
        ],
        out_specs=pl.BlockSpec((1, tile, d), tok),
        out_shape=jax.ShapeDtypeStruct(x.shape, F32),
        scratch_shapes=[
            pltpu.VMEM((tile + 2 * HALO, d), BF16),
            pltpu.VMEM((tile, d), F32),
        ],
        compiler_params=_params("parallel", "parallel"),
        name="conv_ffn",
    )(x, x, x, mod, gain.reshape(1, d), w_up.astype(BF16), w_conv, b_conv.reshape(1, 2 * FF_DIM),
      w_down.astype(BF16), g_final.reshape(1, d))


def _trunk(x, mods, norm_g, a_w_in, a_g_v, a_w_s, a_b_s, a_w_out, b_w_qkv, b_sinks, b_w_o, c_w_in, c_w_out,
           f_w_up, f_w_conv, f_b_conv, f_w_down, g_final):
    for i in range(DEPTH):
        mod = mods[i]
        kind, j = i % N_MIXERS, i // N_MIXERS
        if kind == 0:
            x = _gmlp_layer(x, mod, norm_g[i, 0], a_w_in[j], a_g_v[j], a_w_s[j], a_b_s[j], a_w_out[j])
        elif kind == 1:
            q, k, v = _qkv_proj(x, mod, norm_g[i, 0], b_w_qkv[j])
            x = _attn_layer(x, mod, q, k, v, b_sinks[j], b_w_o[j])
        else:
            x = _fourier_layer(x, mod, norm_g[i, 0], c_w_in[j], c_w_out[j])
        x = _ffn_layer(x, mod, norm_g[i, 1], f_w_up[i], f_w_conv[i], f_b_conv[i], f_w_down[i], g_final,
                       final=(i == DEPTH - 1))
    return x


def kernel(x_prompt, x_sample, c_prompt, c_sample, w_ada, b_ada, norm_g, a_w_in, a_g_v, a_w_s, a_b_s, a_w_out,
           b_w_qkv, b_sinks, b_w_o, c_w_in, c_w_out, f_w_up, f_w_conv, f_b_conv, f_w_down, g_final):
    nb_p, nb_s = x_prompt.shape[0], x_sample.shape[0]
    assert nb_p + nb_s <= MOD_ROWS
    c_all = jnp.concatenate(
        [c_prompt, c_sample, jnp.zeros((MOD_ROWS - nb_p - nb_s, D_MODEL), F32)], axis=0)
    mods = _modulation(c_all, w_ada, b_ada).reshape(DEPTH, MOD_ROWS, 6, D_MODEL)
    weights = (norm_g, a_w_in, a_g_v, a_w_s, a_b_s, a_w_out, b_w_qkv, b_sinks, b_w_o, c_w_in, c_w_out,
               f_w_up, f_w_conv, f_b_conv, f_w_down, g_final)
    y_prompt = _trunk(x_prompt, mods[:, :nb_p], *weights)
    y_sample = _trunk(x_sample, mods[:, nb_p:nb_p + nb_s], *weights)
    return (y_prompt, y_sample)
```

```python
import functools

import numpy as np
import jax
import jax.numpy as jnp
from jax import lax
from jax.experimental import pallas as pl
from jax.experimental.pallas import tpu as pltpu

F32 = jnp.float32
BF16 = jnp.bfloat16

D_MODEL = 1024
DEPTH = 4
N_MIXERS = 3
EPS = 1e-6
NEG_INF = -1e30
CHUNK = 128
A_GROUPS = 8
A_HEAD = D_MODEL // A_GROUPS
B_HEADS = 16
B_KV_HEADS = 4
B_HEAD_DIM = 64
B_Q_PER_KV = B_HEADS // B_KV_HEADS
WINDOW = 128
BLOCK = 128
Q_WIDTH = B_HEADS * B_HEAD_DIM
KV_WIDTH = B_KV_HEADS * B_HEAD_DIM
C_GROUPS = 8
C_GROUP_DIM = D_MODEL // C_GROUPS
FF_DIM = 2816
FF_CHUNK = 256
HALO = 8

MOD_ROWS = 8
MOD_NT = 1536
VMEM_LIMIT = 56 * 1024 * 1024

TOKEN_TILE = 512
FOURIER_TOKENS = 1024


def _params(*sem):
    return pltpu.CompilerParams(dimension_semantics=sem, vmem_limit_bytes=VMEM_LIMIT)


def _norm_mod(x, gain, scale, shift):
    ms = jnp.mean(x * x, axis=-1, keepdims=True)
    return x * lax.rsqrt(ms + EPS) * (gain * (1.0 + scale)) + shift


def _mod_kernel(c_ref, w_ref, b_ref, o_ref):
    c = c_ref[...]
    cs = c * jax.nn.sigmoid(c)
    o_ref[0] = jnp.dot(cs, w_ref[0], preferred_element_type=F32) + b_ref[0]


def _modulation(c_all, w_ada, b_ada):
    n_out = w_ada.shape[-1]
    return pl.pallas_call(
        _mod_kernel,
        grid=(DEPTH, n_out // MOD_NT),
        in_specs=[
            pl.BlockSpec((MOD_ROWS, D_MODEL), lambda i, j: (0, 0)),
            pl.BlockSpec((1, D_MODEL, MOD_NT), lambda i, j: (i, 0, j)),
            pl.BlockSpec((1, 1, MOD_NT), lambda i, j: (i, 0, j)),
        ],
        out_specs=pl.BlockSpec((1, MOD_ROWS, MOD_NT), lambda i, j: (i, 0, j)),
        out_shape=jax.ShapeDtypeStruct((DEPTH, MOD_ROWS, n_out), F32),
        compiler_params=_params("parallel", "parallel"),
        name="modulation",
    )(c_all, w_ada, b_ada.reshape(DEPTH, 1, n_out))


def _gmlp_kernel(x_ref, mod_ref, g_ref, win_ref, gv_ref, ws_ref, bs_ref, wout_ref, o_ref, gated_ref, *, tile):
    x = x_ref[0]
    mod = mod_ref[0]
    h = _norm_mod(x, g_ref[...], mod[1:2], mod[0:1]).astype(BF16)
    uv = jnp.dot(h, win_ref[...], preferred_element_type=F32)
    uv = 0.5 * uv * (1.0 + lax.erf(uv * np.float32(1.0 / np.sqrt(2.0))))
    u = uv[:, :D_MODEL]
    v = uv[:, D_MODEL:]
    v = v * lax.rsqrt(jnp.mean(v * v, axis=-1, keepdims=True) + EPS) * gv_ref[...]
    vb = v.astype(BF16)
    for c in range(tile // CHUNK):
        rows = slice(c * CHUNK, (c + 1) * CHUNK)
        for g in range(A_GROUPS):
            cols = slice(g * A_HEAD, (g + 1) * A_HEAD)
            sv = jnp.dot(ws_ref[g], vb[rows, cols], preferred_element_type=F32) + bs_ref[g]
            gated_ref[rows, cols] = (u[rows, cols] * sv).astype(BF16)
    m = jnp.dot(gated_ref[...], wout_ref[...], preferred_element_type=F32)
    o_ref[0] = x + mod[2:3] * m


def _gmlp_layer(x, mod, gain, w_in, g_v, w_s, b_s, w_out):
    b, s, d = x.shape
    tile = min(TOKEN_TILE, s)
    bs_full = jnp.broadcast_to(b_s[:, :, None], (A_GROUPS, CHUNK, A_HEAD))
    const2 = lambda bi, i: (0, 0)
    const3 = lambda bi, i: (0, 0, 0)
    return pl.pallas_call(
        functools.partial(_gmlp_kernel, tile=tile),
        grid=(b, s // tile),
        in_specs=[
            pl.BlockSpec((1, tile, d), lambda bi, i: (bi, i, 0)),
            pl.BlockSpec((1, 6, d), lambda bi, i: (bi, 0, 0)),
            pl.BlockSpec((1, d), const2),
            pl.BlockSpec((d, 2 * d), const2),
            pl.BlockSpec((1, d), const2),
            pl.BlockSpec((A_GROUPS, CHUNK, CHUNK), const3),
            pl.BlockSpec((A_GROUPS, CHUNK, A_HEAD), const3),
            pl.BlockSpec((d, d), const2),
        ],
        out_specs=pl.BlockSpec((1, tile, d), lambda bi, i: (bi, i, 0)),
        out_shape=jax.ShapeDtypeStruct(x.shape, F32),
        scratch_shapes=[pltpu.VMEM((tile, d), BF16)],
        compiler_params=_params("parallel", "parallel"),
        name="gmlp_mixer",
    )(x, mod, gain.reshape(1, d), w_in.astype(BF16), g_v.reshape(1, d), w_s.astype(BF16), bs_full,
      w_out.astype(BF16))


_PAIRS = B_KV_HEADS // 2
_HEAD_ORDER = [((2 * (t // B_Q_PER_KV) + half) * B_Q_PER_KV + t % B_Q_PER_KV)
               for t in range(B_HEADS // 2) for half in range(2)]
_HEAD_COLS = np.concatenate([np.arange(h * B_HEAD_DIM, (h + 1) * B_HEAD_DIM) for h in _HEAD_ORDER])
GROUP_ROWS = B_Q_PER_KV * BLOCK
BAND = 3 * BLOCK
_NT = (((1,), (1,)), ((), ()))


def _qkv_kernel(x_ref, mod_ref, g_ref, wqk_ref, wvt_ref, q_ref, k_ref, vt_ref, *, tile):
    mod = mod_ref[0]
    h = _norm_mod(x_ref[0], g_ref[...], mod[1:2], mod[0:1]).astype(BF16)
    qk = jnp.dot(h, wqk_ref[...], preferred_element_type=F32)
    for blk in range(tile // BLOCK):
        rows = slice(blk * BLOCK, (blk + 1) * BLOCK)
        for t in range(Q_WIDTH // 128):
            q_ref[0, blk, t * BLOCK:(t + 1) * BLOCK, :] = (
                qk[rows, t * 128:(t + 1) * 128] * (B_HEAD_DIM ** -0.5)).astype(BF16)
    k_ref[0] = qk[:, Q_WIDTH:].astype(BF16)
    vt_ref[0] = lax.dot_general(wvt_ref[...], h, _NT, preferred_element_type=F32).astype(BF16)


def _qkv_proj(x, mod, gain, w_qkv):
    b, s, d = x.shape
    tile = min(TOKEN_TILE, s)
    tok = lambda bi, i: (bi, i, 0)
    const2 = lambda bi, i: (0, 0)
    w_qk = jnp.concatenate([w_qkv[:, _HEAD_COLS], w_qkv[:, Q_WIDTH:Q_WIDTH + KV_WIDTH]], axis=1).astype(BF16)
    w_vt = w_qkv[:, Q_WIDTH + KV_WIDTH:].T.astype(BF16)
    return pl.pallas_call(
        functools.partial(_qkv_kernel, tile=tile),
        grid=(b, s // tile),
        in_specs=[
            pl.BlockSpec((1, tile, d), tok),
            pl.BlockSpec((1, 6, d), lambda bi, i: (bi, 0, 0)),
            pl.BlockSpec((1, d), const2),
            pl.BlockSpec((d, Q_WIDTH + KV_WIDTH), const2),
            pl.BlockSpec((KV_WIDTH, d), const2),
        ],
        out_specs=[
            pl.BlockSpec((1, tile // BLOCK, Q_WIDTH, 128), lambda bi, i: (bi, i, 0, 0)),
            pl.BlockSpec((1, tile, KV_WIDTH), tok),
            pl.BlockSpec((1, KV_WIDTH, tile), lambda bi, i: (bi, 0, i)),
        ],
        out_shape=[
            jax.ShapeDtypeStruct((b, s // BLOCK, Q_WIDTH, 128), BF16),
            jax.ShapeDtypeStruct((b, s, KV_WIDTH), BF16),
            jax.ShapeDtypeStruct((b, KV_WIDTH, s), BF16),
        ],
        compiler_params=_params("parallel", "parallel"),
        name="qkv_proj",
    )(x, mod, gain.reshape(1, d), w_qk, w_vt)


def _attn_bias_table():
    qq = np.arange(BLOCK)[None, :]
    kk = np.arange(BAND)[:, None]
    dist = np.abs(qq + BLOCK - kk)
    slopes = np.exp2(np.float32(-8.0) * np.arange(1, B_HEADS + 1, dtype=np.float32) / np.float32(B_HEADS))
    tbl = np.empty((3, _PAIRS, 2 * BAND, GROUP_ROWS), np.float32)
    for case in range(3):
        valid = dist <= WINDOW
        if case == 0:
            valid = valid & (kk >= BLOCK)
        if case == 2:
            valid = valid & (kk < 2 * BLOCK)
        for p in range(_PAIRS):
            for c in range(B_Q_PER_KV):
                for half in range(2):
                    hd = (2 * p + half) * B_Q_PER_KV + c
                    alibi = (-slopes[hd] * dist.astype(np.float32)).astype(np.float32)
                    tbl[case, p, half * BAND:(half + 1) * BAND, c * BLOCK:(c + 1) * BLOCK] = np.where(
                        valid, alibi, np.float32(NEG_INF))
    return tbl


def _attn_kernel(x_ref, mod_ref, q_ref, kp_ref, k_ref, kn_ref, vp_ref, v_ref, vn_ref, bias_ref, sink_ref, wo_ref,
                 o_ref, kext_ref, vext_ref, ocat_ref, *, tile, nblk):
    i = pl.program_id(1)
    bpt = tile // BLOCK
    kext_ref[0:BLOCK] = kp_ref[0]
    kext_ref[BLOCK:BLOCK + tile] = k_ref[0]
    kext_ref[BLOCK + tile:] = kn_ref[0]
    vext_ref[0] = vp_ref[0]
    for j in range(bpt):
        vext_ref[1 + j] = v_ref[0, :, j * BLOCK:(j + 1) * BLOCK]
    vext_ref[1 + bpt] = vn_ref[0]
    low_lane = lax.broadcasted_iota(jnp.int32, (BAND, 128), 1) < B_HEAD_DIM
    low_row = lax.broadcasted_iota(jnp.int32, (128, BAND), 0) < B_HEAD_DIM
    low_out = lax.broadcasted_iota(jnp.int32, (128, GROUP_ROWS), 0) < B_HEAD_DIM
    ones_row = lax.broadcasted_iota(jnp.int32, (2 * HALO, 2 * BAND), 0) < HALO
    first_half = lax.broadcasted_iota(jnp.int32, (2 * HALO, 2 * BAND), 1) < BAND
    ones = jnp.where(ones_row == first_half, 1.0, 0.0).astype(BF16)

    def q_block(qb, carry):
        r0 = pl.multiple_of(qb * BLOCK, BLOCK)
        n = i * bpt + qb
        case = jnp.where(n == 0, 0, jnp.where(n == nblk - 1, 2, 1))
        for p in range(_PAIRS):
            rows = slice(p * 128, (p + 1) * 128)
            kb = kext_ref[pl.ds(r0, BAND), rows]
            vt = jnp.concatenate([vext_ref[qb, rows, :], vext_ref[qb + 1, rows, :], vext_ref[qb + 2, rows, :]],
                                 axis=1)
            zk = jnp.zeros_like(kb)
            zv = jnp.zeros_like(vt)
            kz = jnp.concatenate([jnp.where(low_lane, kb, zk), jnp.where(low_lane, zk, kb)], axis=0)
            vz = jnp.concatenate([
                jnp.concatenate([jnp.where(low_row, vt, zv), jnp.where(low_row, zv, vt)], axis=1), ones], axis=0)
            q4 = q_ref[0, qb, p * GROUP_ROWS:(p + 1) * GROUP_ROWS, :]
            sc = lax.dot_general(kz, q4, _NT, preferred_element_type=F32) + bias_ref[case, p]
            probs = []
            shift = []
            for half in range(2):
                sh = sc[half * BAND:(half + 1) * BAND]
                sink = sink_ref[p, half]
                mx = jnp.maximum(jnp.max(sh, axis=0, keepdims=True), sink)
                probs.append(jnp.exp(sh - mx).astype(BF16))
                shift.append(jnp.exp(sink - mx))
            pv = jnp.dot(vz, jnp.concatenate(probs, axis=0), preferred_element_type=F32)
            inv_a = 1.0 / (pv[128:129] + shift[0])
            inv_b = 1.0 / (pv[128 + HALO:129 + HALO] + shift[1])
            on = pv[:128] * jnp.where(low_out, inv_a, inv_b)
            for c in range(B_Q_PER_KV):
                t = p * B_Q_PER_KV + c
                ocat_ref[pl.ds(r0, BLOCK), t * 128:(t + 1) * 128] = on[:, c * BLOCK:(c + 1) * BLOCK].T
        return carry

    lax.fori_loop(0, bpt, q_block, 0)
    m = jnp.dot(ocat_ref[...].astype(BF16), wo_ref[...], preferred_element_type=F32)
    o_ref[0] = x_ref[0] + mod_ref[0][2:3] * m


def _attn_layer(x, mod, q, k, vt, sinks, w_o):
    b, s, d = x.shape
    tile = min(TOKEN_TILE, s)
    bpt = tile // BLOCK
    nblk = s // BLOCK
    assert nblk >= 2
    tok = lambda bi, i: (bi, i, 0)
    prev = lambda bi, i: jnp.maximum(i * bpt - 1, 0)
    nxt = lambda bi, i: jnp.minimum((i + 1) * bpt, nblk - 1)
    bias = jnp.asarray(_attn_bias_table())
    head = np.array([[[(2 * p + half) * B_Q_PER_KV + c for c in range(B_Q_PER_KV)] for half in range(2)]
                     for p in range(_PAIRS)])
    sink_tbl = jnp.repeat(sinks.astype(F32)[head], BLOCK, axis=-1)[:, :, None, :]
    return pl.pallas_call(
        functools.partial(_attn_kernel, tile=tile, nblk=nblk),
        grid=(b, s // tile),
        in_specs=[
            pl.BlockSpec((1, tile, d), tok),
            pl.BlockSpec((1, 6, d), lambda bi, i: (bi, 0, 0)),
            pl.BlockSpec((1, bpt, Q_WIDTH, 128), lambda bi, i: (bi, i, 0, 0)),
            pl.BlockSpec((1, BLOCK, KV_WIDTH), lambda bi, i: (bi, prev(bi, i), 0)),
            pl.BlockSpec((1, tile, KV_WIDTH), tok),
            pl.BlockSpec((1, BLOCK, KV_WIDTH), lambda bi, i: (bi, nxt(bi, i), 0)),
            pl.BlockSpec((1, KV_WIDTH, BLOCK), lambda bi, i: (bi, 0, prev(bi, i))),
            pl.BlockSpec((1, KV_WIDTH, tile), lambda bi, i: (bi, 0, i)),
            pl.BlockSpec((1, KV_WIDTH, BLOCK), lambda bi, i: (bi, 0, nxt(bi, i))),
            pl.BlockSpec(bias.shape, lambda bi, i: (0, 0, 0, 0)),
            pl.BlockSpec(sink_tbl.shape, lambda bi, i: (0, 0, 0, 0)),
            pl.BlockSpec((Q_WIDTH, d), lambda bi, i: (0, 0)),
        ],
        out_specs=pl.BlockSpec((1, tile, d), tok),
        out_shape=jax.ShapeDtypeStruct(x.shape, F32),
        scratch_shapes=[
            pltpu.VMEM((tile + 2 * BLOCK, KV_WIDTH), BF16),
            pltpu.VMEM((bpt + 2, KV_WIDTH, BLOCK), BF16),
            pltpu.VMEM((tile, Q_WIDTH), F32),
        ],
        compiler_params=_params("parallel", "parallel"),
        name="window_attention",
    )(x, mod, q, k, k, k, vt, vt, vt, bias, sink_tbl, w_o[_HEAD_COLS].astype(BF16))


def _dft_tables(seq):
    n1 = BLOCK
    n2 = seq // n1
    k1 = np.arange(n1, dtype=np.int64)
    s1 = np.arange(n1, dtype=np.int64)
    s2 = np.arange(n2, dtype=np.int64)
    pos = s1[None, None, :] * n2 + s2[:, None, None]
    ang = 2.0 * np.pi * ((k1[None, :, None] * pos) % seq).astype(np.float64) / seq
    stage1 = np.concatenate([np.cos(ang), -np.sin(ang)], axis=1).astype(np.float32)
    k2 = np.arange(n2, dtype=np.int64)
    ang2 = 2.0 * np.pi * ((k2[:, None] * s2[None, :]) % n2).astype(np.float64) / n2
    c2, sn2 = np.cos(ang2), np.sin(ang2)
    stage2 = np.block([[c2, sn2], [-sn2, c2]]).astype(np.float32)
    c = np.arange(C_GROUP_DIM, dtype=np.int64)
    angc = 2.0 * np.pi * ((c[:, None] * c[None, :]) % C_GROUP_DIM).astype(np.float64) / C_GROUP_DIM
    chan = (np.concatenate([np.cos(angc), np.sin(angc)], axis=0) / np.sqrt(float(seq) * C_GROUP_DIM)).astype(np.float32)
    return n1, n2, stage1, stage2, chan


def _fourier1_kernel(x_ref, mod_ref, g_ref, win_ref, dft_ref, mid_ref, h_ref, *, n1, grp):
    mod = mod_ref[0]
    gain = g_ref[...] * (1.0 + mod[1:2])
    for g in range(grp):
        xg = x_ref[0, :, g * D_MODEL:(g + 1) * D_MODEL]
        ms = jnp.mean(xg * xg, axis=-1, keepdims=True)
        h_ref[g * n1:(g + 1) * n1] = (xg * lax.rsqrt(ms + EPS) * gain + mod[0:1]).astype(BF16)
    z = jnp.dot(h_ref[...], win_ref[...], preferred_element_type=F32).astype(BF16)
    for g in range(grp):
        y = jnp.dot(dft_ref[g], z[g * n1:(g + 1) * n1], preferred_element_type=F32)
        mid_ref[0, 0, g] = y[:n1].astype(BF16)
        mid_ref[0, 1, g] = y[n1:].astype(BF16)


def _fourier2_kernel(x_ref, mod_ref, mid_ref, dft_ref, chan_ref, wout_ref, o_ref, f_ref, *, n2, grp):
    spec = jnp.dot(dft_ref[...], mid_ref[0], preferred_element_type=F32).astype(BF16)
    for kl in range(grp):
        for cg in range(C_GROUPS):
            lo = kl * D_MODEL + cg * C_GROUP_DIM
            lhs = jnp.concatenate([spec[:n2, lo:lo + C_GROUP_DIM], spec[n2:, lo:lo + C_GROUP_DIM]], axis=1)
            f = jnp.dot(lhs, chan_ref[...], preferred_element_type=F32)
            f_ref[kl * n2:(kl + 1) * n2, cg * C_GROUP_DIM:(cg + 1) * C_GROUP_DIM] = f.astype(BF16)
    m = jnp.dot(f_ref[...], wout_ref[...], preferred_element_type=F32)
    gate = mod_ref[0][2:3]
    for kl in range(grp):
        cols = slice(kl * D_MODEL, (kl + 1) * D_MODEL)
        o_ref[0, :, cols] = x_ref[0, :, cols] + gate * m[kl * n2:(kl + 1) * n2]


def _fourier_layer(x, mod, gain, w_in, w_out):
    b, s, d = x.shape
    n1, n2, stage1, stage2, chan = _dft_tables(s)
    g1 = max(1, min(n2, FOURIER_TOKENS // n1))
    g2 = max(1, min(16, FOURIER_TOKENS // n2))
    const2 = lambda bi, j: (0, 0)
    mid = pl.pallas_call(
        functools.partial(_fourier1_kernel, n1=n1, grp=g1),
        grid=(b, n2 // g1),
        in_specs=[
            pl.BlockSpec((1, n1, g1 * d), lambda bi, j: (bi, 0, j)),
            pl.BlockSpec((1, 6, d), lambda bi, j: (bi, 0, 0)),
            pl.BlockSpec((1, d), const2),
            pl.BlockSpec((d, d), const2),
            pl.BlockSpec((g1, 2 * n1, n1), lambda bi, j: (j, 0, 0)),
        ],
        out_specs=pl.BlockSpec((1, 2, g1, n1, d), lambda bi, j: (bi, 0, j, 0, 0)),
        out_shape=jax.ShapeDtypeStruct((b, 2, n2, n1, d), BF16),
        scratch_shapes=[pltpu.VMEM((g1 * n1, d), BF16)],
        compiler_params=_params("parallel", "parallel"),
        name="fourier_stage1",
    )(x.reshape(b, n1, n2 * d), mod, gain.reshape(1, d), w_in.astype(BF16), jnp.asarray(stage1).astype(BF16))
    out = pl.pallas_call(
        functools.partial(_fourier2_kernel, n2=n2, grp=g2),
        grid=(b, n1 // g2),
        in_specs=[
            pl.BlockSpec((1, n2, g2 * d), lambda bi, j: (bi, 0, j)),
            pl.BlockSpec((1, 6, d), lambda bi, j: (bi, 0, 0)),
            pl.BlockSpec((1, 2 * n2, g2 * d), lambda bi, j: (bi, 0, j)),
            pl.BlockSpec((2 * n2, 2 * n2), const2),
            pl.BlockSpec((2 * C_GROUP_DIM, C_GROUP_DIM), const2),
            pl.BlockSpec((d, d), const2),
        ],
        out_specs=pl.BlockSpec((1, n2, g2 * d), lambda bi, j: (bi, 0, j)),
        out_shape=jax.ShapeDtypeStruct((b, n2, n1 * d), F32),
        scratch_shapes=[pltpu.VMEM((g2 * n2, d), BF16)],
        compiler_params=_params("parallel", "parallel"),
        name="fourier_stage2",
    )(x.reshape(b, n2, n1 * d), mod, mid.reshape(b, 2 * n2, n1 * d), jnp.asarray(stage2).astype(BF16),
      jnp.asarray(chan).astype(BF16), w_out.astype(BF16))
    return out.reshape(b, s, d)


def _ffn_kernel(xp_ref, x_ref, xn_ref, mod_ref, g_ref, wup_ref, wc_ref, bc_ref, wdn_ref, gf_ref, o_ref,
                h_ref, up_ref, act_ref, *, tile, final):
    i = pl.program_id(1)
    last = pl.num_programs(1) - 1
    mod = mod_ref[0]
    gain = g_ref[...]
    x = pltpu.einshape("(rq)d->(qr)d", x_ref[0], r=HALO)
    h_ref[0:tile] = _norm_mod(x, gain, mod[4:5], mod[3:4]).astype(BF16)
    h_next = _norm_mod(xn_ref[0], gain, mod[4:5], mod[3:4]) * (i < last).astype(F32)
    h_prev = _norm_mod(xp_ref[0], gain, mod[4:5], mod[3:4]) * (i > 0).astype(F32)
    h_ref[tile:] = jnp.concatenate([h_next, h_prev], axis=0).astype(BF16)
    h = h_ref[...]
    n_chunks = FF_DIM // FF_CHUNK
    sub = lax.broadcasted_iota(jnp.int32, (HALO, FF_CHUNK), 0)

    def up_proj(j):
        for half in range(2):
            c0 = half * FF_DIM + j * FF_CHUNK
            up = jnp.dot(h, wup_ref[:, c0:c0 + FF_CHUNK], preferred_element_type=F32)
            dst = up_ref.at[j % 2, half]
            dst[HALO:HALO + tile] = up[:tile]
            dst[0:HALO] = jnp.where(sub == 0, pltpu.roll(up[tile + HALO:], 1, 0),
                                    pltpu.roll(up[tile - HALO:tile], 1, 0))
            dst[HALO + tile:] = jnp.where(sub == HALO - 1, pltpu.roll(up[tile:tile + HALO], HALO - 1, 0),
                                          pltpu.roll(up[0:HALO], HALO - 1, 0))

    def conv(j, half):
        cols = slice(half * FF_DIM + j * FF_CHUNK, half * FF_DIM + (j + 1) * FF_CHUNK)
        w = wc_ref[:, cols]
        src = up_ref.at[j % 2, half]
        return (src[0:tile] * w[0:1] + src[HALO:HALO + tile] * w[1:2] + src[2 * HALO:2 * HALO + tile] * w[2:3]
                + bc_ref[:, cols])

    up_proj(0)
    for j in range(n_chunks):
        if j + 1 < n_chunks:
            up_proj(j + 1)
        a = conv(j, 0)
        g = conv(j, 1)
        act_ref[:, j * FF_CHUNK:(j + 1) * FF_CHUNK] = (a * (g * jax.nn.sigmoid(g))).astype(BF16)
    y = x + mod[5:6] * jnp.dot(act_ref[...], wdn_ref[...], preferred_element_type=F32)
    if final:
        y = y * lax.rsqrt(jnp.mean(y * y, axis=-1, keepdims=True) + EPS) * gf_ref[...]
    o_ref[0] = pltpu.einshape("(qr)d->(rq)d", y, r=HALO)


def _ffn_layer(x, mod, gain, w_up, w_conv, b_conv, w_down, g_final, final):
    b, s, d = x.shape
    tile = min(TOKEN_TILE, s)
    hpt = tile // HALO
    nh = s // HALO
    tok = lambda bi, i: (bi, i, 0)
    const2 = lambda bi, i: (0, 0)
    return pl.pallas_call(
        functools.partial(_ffn_kernel, tile=tile, final=final),
        grid=(b, s // tile),
        in_specs=[
            pl.BlockSpec((1, HALO, d), lambda bi, i: (bi, jnp.maximum(i * hpt - 1, 0), 0)),
            pl.BlockSpec((1, tile, d), tok),
            pl.BlockSpec((1, HALO, d), lambda bi, i: (bi, jnp.minimum((i + 1) * hpt, nh - 1), 0)),
            pl.BlockSpec((1, 6, d), lambda bi, i: (bi, 0, 0)),
            pl.BlockSpec((1, d), const2),
            pl.BlockSpec((d, 2 * FF_DIM), const2),
            pl.BlockSpec((3, 2 * FF_DIM), const2),
            pl.BlockSpec((1, 2 * FF_DIM), const2),
            pl.BlockSpec((FF_DIM, d), const2),
            pl.BlockSpec((1, d), const2),
        ],
        out_specs=pl.BlockSpec((1, tile, d), tok),
        out_shape=jax.ShapeDtypeStruct(x.shape, F32),
        scratch_shapes=[
            pltpu.VMEM((tile + 2 * HALO, d), BF16),
            pltpu.VMEM((2, 2, tile + 2 * HALO, FF_CHUNK), F32),
            pltpu.VMEM((tile, FF_DIM), BF16),
        ],
        compiler_params=_params("parallel", "parallel"),
        name="conv_ffn",
    )(x, x, x, mod, gain.reshape(1, d), w_up.astype(BF16), w_conv, b_conv.reshape(1, 2 * FF_DIM),
      w_down.astype(BF16), g_final.reshape(1, d))


def _trunk(x, mods, norm_g, a_w_in, a_g_v, a_w_s, a_b_s, a_w_out, b_w_qkv, b_sinks, b_w_o, c_w_in, c_w_out,
           f_w_up, f_w_conv, f_b_conv, f_w_down, g_final):
    for i in range(DEPTH):
        mod = mods[i]
        kind, j = i % N_MIXERS, i // N_MIXERS
        if kind == 0:
            x = _gmlp_layer(x, mod, norm_g[i, 0], a_w_in[j], a_g_v[j], a_w_s[j], a_b_s[j], a_w_out[j])
        elif kind == 1:
            q, k, v = _qkv_proj(x, mod, norm_g[i, 0], b_w_qkv[j])
            x = _attn_layer(x, mod, q, k, v, b_sinks[j], b_w_o[j])
        else:
            x = _fourier_layer(x, mod, norm_g[i, 0], c_w_in[j], c_w_out[j])
        x = _ffn_layer(x, mod, norm_g[i, 1], f_w_up[i], f_w_conv[i], f_b_conv[i], f_w_down[i], g_final,
                       final=(i == DEPTH - 1))
    return x


def kernel(x_prompt, x_sample, c_prompt, c_sample, w_ada, b_ada, norm_g, a_w_in, a_g_v, a_w_s, a_b_s, a_w_out,
           b_w_qkv, b_sinks, b_w_o, c_w_in, c_w_out, f_w_up, f_w_conv, f_b_conv, f_w_down, g_final):
    nb_p, nb_s = x_prompt.shape[0], x_sample.shape[0]
    assert nb_p + nb_s <= MOD_ROWS
    c_all = jnp.concatenate(
        [c_prompt, c_sample, jnp.zeros((MOD_ROWS - nb_p - nb_s, D_MODEL), F32)], axis=0)
    mods = _modulation(c_all, w_ada, b_ada).reshape(DEPTH, MOD_ROWS, 6, D_MODEL)
    weights = (norm_g, a_w_in, a_g_v, a_w_s, a_b_s, a_w_out, b_w_qkv, b_sinks, b_w_o, c_w_in, c_w_out,
               f_w_up, f_w_conv, f_b_conv, f_w_down, g_final)
    y_prompt = _trunk(x_prompt, mods[:, :nb_p], *weights)
    y_sample = _trunk(x_sample, mods[:, nb_p:nb_p + nb_s], *weights)
    return (y_prompt, y_sample)
```

```python
import functools

import numpy as np
import jax
import jax.numpy as jnp
from jax import lax
from jax.experimental import pallas as pl
from jax.experimental.pallas import tpu as pltpu

F32 = jnp.float32
BF16 = jnp.bfloat16

D_MODEL = 1024
DEPTH = 4
N_MIXERS = 3
EPS = 1e-6
NEG_INF = -1e30
CHUNK = 128
A_GROUPS = 8
A_HEAD = D_MODEL // A_GROUPS
B_HEADS = 16
B_KV_HEADS = 4
B_HEAD_DIM = 64
B_Q_PER_KV = B_HEADS // B_KV_HEADS
WINDOW = 128
BLOCK = 128
Q_WIDTH = B_HEADS * B_HEAD_DIM
KV_WIDTH = B_KV_HEADS * B_HEAD_DIM
C_GROUPS = 8
C_GROUP_DIM = D_MODEL // C_GROUPS
FF_DIM = 2816
FF_CHUNK = 256
HALO = 8

MOD_ROWS = 8
MOD_NT = 1536
VMEM_LIMIT = 56 * 1024 * 1024

TOKEN_TILE = 512
FOURIER_TOKENS = 1024


def _params(*sem):
    return pltpu.CompilerParams(dimension_semantics=sem, vmem_limit_bytes=VMEM_LIMIT)


def _norm_mod(x, gain, scale, shift):
    ms = jnp.mean(x * x, axis=-1, keepdims=True)
    return x * lax.rsqrt(ms + EPS) * (gain * (1.0 + scale)) + shift


def _regroup_rows(x, outer):
    rows, cols = x.shape
    return jnp.transpose(x.reshape(outer, rows // outer, cols), (1, 0, 2)).reshape(rows, cols)


def _mod_kernel(c_ref, w_ref, b_ref, o_ref):
    c = c_ref[...]
    cs = c * jax.nn.sigmoid(c)
    o_ref[0] = jnp.dot(cs, w_ref[0], preferred_element_type=F32) + b_ref[0]


def _modulation(c_all, w_ada, b_ada):
    n_out = w_ada.shape[-1]
    return pl.pallas_call(
        _mod_kernel,
        grid=(DEPTH, n_out // MOD_NT),
        in_specs=[
            pl.BlockSpec((MOD_ROWS, D_MODEL), lambda i, j: (0, 0)),
            pl.BlockSpec((1, D_MODEL, MOD_NT), lambda i, j: (i, 0, j)),
            pl.BlockSpec((1, 1, MOD_NT), lambda i, j: (i, 0, j)),
        ],
        out_specs=pl.BlockSpec((1, MOD_ROWS, MOD_NT), lambda i, j: (i, 0, j)),
        out_shape=jax.ShapeDtypeStruct((DEPTH, MOD_ROWS, n_out), F32),
        compiler_params=_params("parallel", "parallel"),
        name="modulation",
    )(c_all, w_ada, b_ada.reshape(DEPTH, 1, n_out))


def _gmlp_kernel(x_ref, mod_ref, g_ref, win_ref, gv_ref, ws_ref, bs_ref, wout_ref, o_ref, gated_ref, *, tile):
    x = x_ref[0]
    mod = mod_ref[0]
    h = _norm_mod(x, g_ref[...], mod[1:2], mod[0:1]).astype(BF16)
    uv = jnp.dot(h, win_ref[...], preferred_element_type=F32)
    uv = 0.5 * uv * (1.0 + lax.erf(uv * np.float32(1.0 / np.sqrt(2.0))))
    u = uv[:, :D_MODEL]
    v = uv[:, D_MODEL:]
    v = v * lax.rsqrt(jnp.mean(v * v, axis=-1, keepdims=True) + EPS) * gv_ref[...]
    vb = v.astype(BF16)
    for c in range(tile // CHUNK):
        rows = slice(c * CHUNK, (c + 1) * CHUNK)
        for g in range(A_GROUPS):
            cols = slice(g * A_HEAD, (g + 1) * A_HEAD)
            sv = jnp.dot(ws_ref[g], vb[rows, cols], preferred_element_type=F32) + bs_ref[g]
            gated_ref[rows, cols] = (u[rows, cols] * sv).astype(BF16)
    m = jnp.dot(gated_ref[...], wout_ref[...], preferred_element_type=F32)
    o_ref[0] = x + mod[2:3] * m


def _gmlp_layer(x, mod, gain, w_in, g_v, w_s, b_s, w_out):
    b, s, d = x.shape
    tile = min(TOKEN_TILE, s)
    bs_full = jnp.broadcast_to(b_s[:, :, None], (A_GROUPS, CHUNK, A_HEAD))
    const2 = lambda bi, i: (0, 0)
    const3 = lambda bi, i: (0, 0, 0)
    return pl.pallas_call(
        functools.partial(_gmlp_kernel, tile=tile),
        grid=(b, s // tile),
        in_specs=[
            pl.BlockSpec((1, tile, d), lambda bi, i: (bi, i, 0)),
            pl.BlockSpec((1, 6, d), lambda bi, i: (bi, 0, 0)),
            pl.BlockSpec((1, d), const2),
            pl.BlockSpec((d, 2 * d), const2),
            pl.BlockSpec((1, d), const2),
            pl.BlockSpec((A_GROUPS, CHUNK, CHUNK), const3),
            pl.BlockSpec((A_GROUPS, CHUNK, A_HEAD), const3),
            pl.BlockSpec((d, d), const2),
        ],
        out_specs=pl.BlockSpec((1, tile, d), lambda bi, i: (bi, i, 0)),
        out_shape=jax.ShapeDtypeStruct(x.shape, F32),
        scratch_shapes=[pltpu.VMEM((tile, d), BF16)],
        compiler_params=_params("parallel", "parallel"),
        name="gmlp_mixer",
    )(x, mod, gain.reshape(1, d), w_in.astype(BF16), g_v.reshape(1, d), w_s.astype(BF16), bs_full,
      w_out.astype(BF16))


_PAIRS = B_KV_HEADS // 2
_HEAD_ORDER = [((2 * (t // B_Q_PER_KV) + half) * B_Q_PER_KV + t % B_Q_PER_KV)
               for t in range(B_HEADS // 2) for half in range(2)]
_HEAD_COLS = np.concatenate([np.arange(h * B_HEAD_DIM, (h + 1) * B_HEAD_DIM) for h in _HEAD_ORDER])
GROUP_ROWS = B_Q_PER_KV * BLOCK
BAND = 3 * BLOCK
_NT = (((1,), (1,)), ((), ()))


def _qkv_kernel(x_ref, mod_ref, g_ref, wqk_ref, wvt_ref, q_ref, k_ref, vt_ref, *, tile):
    mod = mod_ref[0]
    h = _norm_mod(x_ref[0], g_ref[...], mod[1:2], mod[0:1]).astype(BF16)
    qk = jnp.dot(h, wqk_ref[...], preferred_element_type=F32)
    for blk in range(tile // BLOCK):
        rows = slice(blk * BLOCK, (blk + 1) * BLOCK)
        for t in range(Q_WIDTH // 128):
            q_ref[0, blk, t * BLOCK:(t + 1) * BLOCK, :] = (
                qk[rows, t * 128:(t + 1) * 128] * (B_HEAD_DIM ** -0.5)).astype(BF16)
    k_ref[0] = qk[:, Q_WIDTH:].astype(BF16)
    vt_ref[0] = lax.dot_general(wvt_ref[...], h, _NT, preferred_element_type=F32).astype(BF16)


def _qkv_proj(x, mod, gain, w_qkv):
    b, s, d = x.shape
    tile = min(TOKEN_TILE, s)
    tok = lambda bi, i: (bi, i, 0)
    const2 = lambda bi, i: (0, 0)
    w_qk = jnp.concatenate([w_qkv[:, _HEAD_COLS], w_qkv[:, Q_WIDTH:Q_WIDTH + KV_WIDTH]], axis=1).astype(BF16)
    w_vt = w_qkv[:, Q_WIDTH + KV_WIDTH:].T.astype(BF16)
    return pl.pallas_call(
        functools.partial(_qkv_kernel, tile=tile),
        grid=(b, s // tile),
        in_specs=[
            pl.BlockSpec((1, tile, d), tok),
            pl.BlockSpec((1, 6, d), lambda bi, i: (bi, 0, 0)),
            pl.BlockSpec((1, d), const2),
            pl.BlockSpec((d, Q_WIDTH + KV_WIDTH), const2),
            pl.BlockSpec((KV_WIDTH, d), const2),
        ],
        out_specs=[
            pl.BlockSpec((1, tile // BLOCK, Q_WIDTH, 128), lambda bi, i: (bi, i, 0, 0)),
            pl.BlockSpec((1, tile, KV_WIDTH), tok),
            pl.BlockSpec((1, KV_WIDTH, tile), lambda bi, i: (bi, 0, i)),
        ],
        out_shape=[
            jax.ShapeDtypeStruct((b, s // BLOCK, Q_WIDTH, 128), BF16),
            jax.ShapeDtypeStruct((b, s, KV_WIDTH), BF16),
            jax.ShapeDtypeStruct((b, KV_WIDTH, s), BF16),
        ],
        compiler_params=_params("parallel", "parallel"),
        name="qkv_proj",
    )(x, mod, gain.reshape(1, d), w_qk, w_vt)


def _attn_bias_table():
    qq = np.arange(BLOCK)[None, :]
    kk = np.arange(BAND)[:, None]
    dist = np.abs(qq + BLOCK - kk)
    slopes = np.exp2(np.float32(-8.0) * np.arange(1, B_HEADS + 1, dtype=np.float32) / np.float32(B_HEADS))
    tbl = np.empty((3, _PAIRS, 2 * BAND, GROUP_ROWS), np.float32)
    for case in range(3):
        valid = dist <= WINDOW
        if case == 0:
            valid = valid & (kk >= BLOCK)
        if case == 2:
            valid = valid & (kk < 2 * BLOCK)
        for p in range(_PAIRS):
            for c in range(B_Q_PER_KV):
                for half in range(2):
                    hd = (2 * p + half) * B_Q_PER_KV + c
                    alibi = (-slopes[hd] * dist.astype(np.float32)).astype(np.float32)
                    tbl[case, p, half * BAND:(half + 1) * BAND, c * BLOCK:(c + 1) * BLOCK] = np.where(
                        valid, alibi, np.float32(NEG_INF))
    return tbl


def _attn_kernel(x_ref, mod_ref, q_ref, kp_ref, k_ref, kn_ref, vp_ref, v_ref, vn_ref, bias_ref, sink_ref, wo_ref,
                 o_ref, kext_ref, vext_ref, ocat_ref, *, tile, nblk):
    i = pl.program_id(1)
    bpt = tile // BLOCK
    kext_ref[0:BLOCK] = kp_ref[0]
    kext_ref[BLOCK:BLOCK + tile] = k_ref[0]
    kext_ref[BLOCK + tile:] = kn_ref[0]
    vext_ref[0] = vp_ref[0]
    for j in range(bpt):
        vext_ref[1 + j] = v_ref[0, :, j * BLOCK:(j + 1) * BLOCK]
    vext_ref[1 + bpt] = vn_ref[0]
    low_lane = lax.broadcasted_iota(jnp.int32, (BAND, 128), 1) < B_HEAD_DIM
    low_row = lax.broadcasted_iota(jnp.int32, (128, BAND), 0) < B_HEAD_DIM
    low_out = lax.broadcasted_iota(jnp.int32, (128, GROUP_ROWS), 0) < B_HEAD_DIM
    ones_row = lax.broadcasted_iota(jnp.int32, (2 * HALO, 2 * BAND), 0) < HALO
    first_half = lax.broadcasted_iota(jnp.int32, (2 * HALO, 2 * BAND), 1) < BAND
    ones = jnp.where(ones_row == first_half, 1.0, 0.0).astype(BF16)

    def q_block(qb, carry):
        r0 = pl.multiple_of(qb * BLOCK, BLOCK)
        n = i * bpt + qb
        case = jnp.where(n == 0, 0, jnp.where(n == nblk - 1, 2, 1))
        for p in range(_PAIRS):
            rows = slice(p * 128, (p + 1) * 128)
            kb = kext_ref[pl.ds(r0, BAND), rows]
            vt = jnp.concatenate([vext_ref[qb, rows, :], vext_ref[qb + 1, rows, :], vext_ref[qb + 2, rows, :]],
                                 axis=1)
            zk = jnp.zeros_like(kb)
            zv = jnp.zeros_like(vt)
            kz = jnp.concatenate([jnp.where(low_lane, kb, zk), jnp.where(low_lane, zk, kb)], axis=0)
            vz = jnp.concatenate([
                jnp.concatenate([jnp.where(low_row, vt, zv), jnp.where(low_row, zv, vt)], axis=1), ones], axis=0)
            q4 = q_ref[0, qb, p * GROUP_ROWS:(p + 1) * GROUP_ROWS, :]
            sc = lax.dot_general(kz, q4, _NT, preferred_element_type=F32) + bias_ref[case, p]
            probs = []
            shift = []
            for half in range(2):
                sh = sc[half * BAND:(half + 1) * BAND]
                sink = sink_ref[p, half]
                mx = jnp.maximum(jnp.max(sh, axis=0, keepdims=True), sink)
                probs.append(jnp.exp(sh - mx).astype(BF16))
                shift.append(jnp.exp(sink - mx))
            pv = jnp.dot(vz, jnp.concatenate(probs, axis=0), preferred_element_type=F32)
            inv_a = 1.0 / (pv[128:129] + shift[0])
            inv_b = 1.0 / (pv[128 + HALO:129 + HALO] + shift[1])
            on = pv[:128] * jnp.where(low_out, inv_a, inv_b)
            for c in range(B_Q_PER_KV):
                t = p * B_Q_PER_KV + c
                ocat_ref[pl.ds(r0, BLOCK), t * 128:(t + 1) * 128] = on[:, c * BLOCK:(c + 1) * BLOCK].T
        return carry

    lax.fori_loop(0, bpt, q_block, 0)
    m = jnp.dot(ocat_ref[...].astype(BF16), wo_ref[...], preferred_element_type=F32)
    o_ref[0] = x_ref[0] + mod_ref[0][2:3] * m


def _attn_layer(x, mod, q, k, vt, sinks, w_o):
    b, s, d = x.shape
    tile = min(TOKEN_TILE, s)
    bpt = tile // BLOCK
    nblk = s // BLOCK
    assert nblk >= 2
    tok = lambda bi, i: (bi, i, 0)
    prev = lambda bi, i: jnp.maximum(i * bpt - 1, 0)
    nxt = lambda bi, i: jnp.minimum((i + 1) * bpt, nblk - 1)
    bias = jnp.asarray(_attn_bias_table())
    head = np.array([[[(2 * p + half) * B_Q_PER_KV + c for c in range(B_Q_PER_KV)] for half in range(2)]
                     for p in range(_PAIRS)])
    sink_tbl = jnp.repeat(sinks.astype(F32)[head], BLOCK, axis=-1)[:, :, None, :]
    return pl.pallas_call(
        functools.partial(_attn_kernel, tile=tile, nblk=nblk),
        grid=(b, s // tile),
        in_specs=[
            pl.BlockSpec((1, tile, d), tok),
            pl.BlockSpec((1, 6, d), lambda bi, i: (bi, 0, 0)),
            pl.BlockSpec((1, bpt, Q_WIDTH, 128), lambda bi, i: (bi, i, 0, 0)),
            pl.BlockSpec((1, BLOCK, KV_WIDTH), lambda bi, i: (bi, prev(bi, i), 0)),
            pl.BlockSpec((1, tile, KV_WIDTH), tok),
            pl.BlockSpec((1, BLOCK, KV_WIDTH), lambda bi, i: (bi, nxt(bi, i), 0)),
            pl.BlockSpec((1, KV_WIDTH, BLOCK), lambda bi, i: (bi, 0, prev(bi, i))),
            pl.BlockSpec((1, KV_WIDTH, tile), lambda bi, i: (bi, 0, i)),
            pl.BlockSpec((1, KV_WIDTH, BLOCK), lambda bi, i: (bi, 0, nxt(bi, i))),
            pl.BlockSpec(bias.shape, lambda bi, i: (0, 0, 0, 0)),
            pl.BlockSpec(sink_tbl.shape, lambda bi, i: (0, 0, 0, 0)),
            pl.BlockSpec((Q_WIDTH, d), lambda bi, i: (0, 0)),
        ],
        out_specs=pl.BlockSpec((1, tile, d), tok),
        out_shape=jax.ShapeDtypeStruct(x.shape, F32),
        scratch_shapes=[
            pltpu.VMEM((tile + 2 * BLOCK, KV_WIDTH), BF16),
            pltpu.VMEM((bpt + 2, KV_WIDTH, BLOCK), BF16),
            pltpu.VMEM((tile, Q_WIDTH), F32),
        ],
        compiler_params=_params("parallel", "parallel"),
        name="window_attention",
    )(x, mod, q, k, k, k, vt, vt, vt, bias, sink_tbl, w_o[_HEAD_COLS].astype(BF16))


def _dft_tables(seq):
    n1 = BLOCK
    n2 = seq // n1
    k1 = np.arange(n1, dtype=np.int64)
    s1 = np.arange(n1, dtype=np.int64)
    s2 = np.arange(n2, dtype=np.int64)
    pos = s1[None, None, :] * n2 + s2[:, None, None]
    ang = 2.0 * np.pi * ((k1[None, :, None] * pos) % seq).astype(np.float64) / seq
    stage1 = np.concatenate([np.cos(ang), -np.sin(ang)], axis=1).astype(np.float32)
    k2 = np.arange(n2, dtype=np.int64)
    ang2 = 2.0 * np.pi * ((k2[:, None] * s2[None, :]) % n2).astype(np.float64) / n2
    c2, sn2 = np.cos(ang2), np.sin(ang2)
    stage2 = np.block([[c2, sn2], [-sn2, c2]]).astype(np.float32)
    c = np.arange(C_GROUP_DIM, dtype=np.int64)
    angc = 2.0 * np.pi * ((c[:, None] * c[None, :]) % C_GROUP_DIM).astype(np.float64) / C_GROUP_DIM
    chan = (np.concatenate([np.cos(angc), np.sin(angc)], axis=0) / np.sqrt(float(seq) * C_GROUP_DIM)).astype(np.float32)
    return n1, n2, stage1, stage2, chan


def _fourier1_kernel(x_ref, mod_ref, g_ref, win_ref, dft_ref, mid_ref, *, n1, grp):
    mod = mod_ref[0]
    x = x_ref[0].reshape(n1 * grp, D_MODEL)
    h = _norm_mod(x, g_ref[...], mod[1:2], mod[0:1]).astype(BF16)
    z = jnp.dot(h, win_ref[...], preferred_element_type=F32)
    z = _regroup_rows(z, n1).astype(BF16)
    for g in range(grp):
        y = jnp.dot(dft_ref[g], z[g * n1:(g + 1) * n1], preferred_element_type=F32)
        mid_ref[0, 0, g] = y[:n1].astype(BF16)
        mid_ref[0, 1, g] = y[n1:].astype(BF16)


def _fourier2_kernel(x_ref, mod_ref, mid_ref, dft_ref, chan_ref, wout_ref, o_ref, f_ref, *, n2, grp):
    spec = jnp.dot(dft_ref[...], mid_ref[0], preferred_element_type=F32).astype(BF16)
    for kl in range(grp):
        for cg in range(C_GROUPS):
            lo = kl * D_MODEL + cg * C_GROUP_DIM
            lhs = jnp.concatenate([spec[:n2, lo:lo + C_GROUP_DIM], spec[n2:, lo:lo + C_GROUP_DIM]], axis=1)
            f = jnp.dot(lhs, chan_ref[...], preferred_element_type=F32)
            f_ref[kl * n2:(kl + 1) * n2, cg * C_GROUP_DIM:(cg + 1) * C_GROUP_DIM] = f.astype(BF16)
    m = jnp.dot(f_ref[...], wout_ref[...], preferred_element_type=F32)
    m = _regroup_rows(m, grp)
    out = x_ref[0].reshape(n2 * grp, D_MODEL) + mod_ref[0][2:3] * m
    o_ref[0] = out.reshape(n2, grp, D_MODEL)


def _fourier_layer(x, mod, gain, w_in, w_out):
    b, s, d = x.shape
    n1, n2, stage1, stage2, chan = _dft_tables(s)
    grp = HALO
    const2 = lambda bi, j: (0, 0)
    mid = pl.pallas_call(
        functools.partial(_fourier1_kernel, n1=n1, grp=grp),
        grid=(b, n2 // grp),
        in_specs=[
            pl.BlockSpec((1, n1, grp, d), lambda bi, j: (bi, 0, j, 0)),
            pl.BlockSpec((1, 6, d), lambda bi, j: (bi, 0, 0)),
            pl.BlockSpec((1, d), const2),
            pl.BlockSpec((d, d), const2),
            pl.BlockSpec((grp, 2 * n1, n1), lambda bi, j: (j, 0, 0)),
        ],
        out_specs=pl.BlockSpec((1, 2, grp, n1, d), lambda bi, j: (bi, 0, j, 0, 0)),
        out_shape=jax.ShapeDtypeStruct((b, 2, n2, n1, d), BF16),
        compiler_params=_params("parallel", "parallel"),
        name="fourier_stage1",
    )(x.reshape(b, n1, n2, d), mod, gain.reshape(1, d), w_in.astype(BF16), jnp.asarray(stage1).astype(BF16))
    out = pl.pallas_call(
        functools.partial(_fourier2_kernel, n2=n2, grp=grp),
        grid=(b, n1 // grp),
        in_specs=[
            pl.BlockSpec((1, n2, grp, d), lambda bi, j: (bi, 0, j, 0)),
            pl.BlockSpec((1, 6, d), lambda bi, j: (bi, 0, 0)),
            pl.BlockSpec((1, 2 * n2, grp * d), lambda bi, j: (bi, 0, j)),
            pl.BlockSpec((2 * n2, 2 * n2), const2),
            pl.BlockSpec((2 * C_GROUP_DIM, C_GROUP_DIM), const2),
            pl.BlockSpec((d, d), const2),
        ],
        out_specs=pl.BlockSpec((1, n2, grp, d), lambda bi, j: (bi, 0, j, 0)),
        out_shape=jax.ShapeDtypeStruct((b, n2, n1, d), F32),
        scratch_shapes=[pltpu.VMEM((grp * n2, d), BF16)],
        compiler_params=_params("parallel", "parallel"),
        name="fourier_stage2",
    )(x.reshape(b, n2, n1, d), mod, mid.reshape(b, 2 * n2, n1 * d), jnp.asarray(stage2).astype(BF16),
      jnp.asarray(chan).astype(BF16), w_out.astype(BF16))
    return out.reshape(b, s, d)


def _ffn_kernel(xp_ref, x_ref, xn_ref, mod_ref, g_ref, wup_ref, wc_ref, bc_ref, wdn_ref, gf_ref, o_ref,
                h_ref, up_ref, act_ref, *, tile, final):
    i = pl.program_id(1)
    last = pl.num_programs(1) - 1
    mod = mod_ref[0]
    gain = g_ref[...]
    x = _regroup_rows(x_ref[0], HALO)
    h_ref[0:tile] = _norm_mod(x, gain, mod[4:5], mod[3:4]).astype(BF16)
    h_next = _norm_mod(xn_ref[0], gain, mod[4:5], mod[3:4]) * (i < last).astype(F32)
    h_prev = _norm_mod(xp_ref[0], gain, mod[4:5], mod[3:4]) * (i > 0).astype(F32)
    h_ref[tile:] = jnp.concatenate([h_next, h_prev], axis=0).astype(BF16)
    h = h_ref[...]
    n_chunks = FF_DIM // FF_CHUNK
    sub = lax.broadcasted_iota(jnp.int32, (HALO, FF_CHUNK), 0)

    def up_proj(j):
        for half in range(2):
            c0 = half * FF_DIM + j * FF_CHUNK
            up = jnp.dot(h, wup_ref[:, c0:c0 + FF_CHUNK], preferred_element_type=F32)
            dst = up_ref.at[j % 2, half]
            dst[HALO:HALO + tile] = up[:tile]
            dst[0:HALO] = jnp.where(sub == 0, pltpu.roll(up[tile + HALO:], 1, 0),
                                    pltpu.roll(up[tile - HALO:tile], 1, 0))
            dst[HALO + tile:] = jnp.where(sub == HALO - 1, pltpu.roll(up[tile:tile + HALO], HALO - 1, 0),
                                          pltpu.roll(up[0:HALO], HALO - 1, 0))

    def conv(j, half):
        cols = slice(half * FF_DIM + j * FF_CHUNK, half * FF_DIM + (j + 1) * FF_CHUNK)
        w = wc_ref[:, cols]
        src = up_ref.at[j % 2, half]
        return (src[0:tile] * w[0:1] + src[HALO:HALO + tile] * w[1:2] + src[2 * HALO:2 * HALO + tile] * w[2:3]
                + bc_ref[:, cols])

    up_proj(0)
    for j in range(n_chunks):
        if j + 1 < n_chunks:
            up_proj(j + 1)
        a = conv(j, 0)
        g = conv(j, 1)
        act_ref[:, j * FF_CHUNK:(j + 1) * FF_CHUNK] = (a * (g * jax.nn.sigmoid(g))).astype(BF16)
    y = x + mod[5:6] * jnp.dot(act_ref[...], wdn_ref[...], preferred_element_type=F32)
    if final:
        y = y * lax.rsqrt(jnp.mean(y * y, axis=-1, keepdims=True) + EPS) * gf_ref[...]
    o_ref[0] = _regroup_rows(y, tile // HALO)


def _ffn_layer(x, mod, gain, w_up, w_conv, b_conv, w_down, g_final, final):
    b, s, d = x.shape
    tile = min(TOKEN_TILE, s)
    hpt = tile // HALO
    nh = s // HALO
    tok = lambda bi, i: (bi, i, 0)
    const2 = lambda bi, i: (0, 0)
    return pl.pallas_call(
        functools.partial(_ffn_kernel, tile=tile, final=final),
        grid=(b, s // tile),
        in_specs=[
            pl.BlockSpec((1, HALO, d), lambda bi, i: (bi, jnp.maximum(i * hpt - 1, 0), 0)),
            pl.BlockSpec((1, tile, d), tok),
            pl.BlockSpec((1, HALO, d), lambda bi, i: (bi, jnp.minimum((i + 1) * hpt, nh - 1), 0)),
            pl.BlockSpec((1, 6, d), lambda bi, i: (bi, 0, 0)),
            pl.BlockSpec((1, d), const2),
            pl.BlockSpec((d, 2 * FF_DIM), const2),
            pl.BlockSpec((3, 2 * FF_DIM), const2),
            pl.BlockSpec((1, 2 * FF_DIM), const2),
            pl.BlockSpec((FF_DIM, d), const2),
            pl.BlockSpec((1, d), const2),
        ],
        out_specs=pl.BlockSpec((1, tile, d), tok),
        out_shape=jax.ShapeDtypeStruct(x.shape, F32),
        scratch_shapes=[
            pltpu.VMEM((tile + 2 * HALO, d), BF16),
            pltpu.VMEM((2, 2, tile + 2 * HALO, FF_CHUNK), F32),
            pltpu.VMEM((tile, FF_DIM), BF16),
        ],
        compiler_params=_params("parallel", "parallel"),
        name="conv_ffn",
    )(x, x, x, mod, gain.reshape(1, d), w_up.astype(BF16), w_conv, b_conv.reshape(1, 2 * FF_DIM),
      w_down.astype(BF16), g_final.reshape(1, d))


def _trunk(x, mods, norm_g, a_w_in, a_g_v, a_w_s, a_b_s, a_w_out, b_w_qkv, b_sinks, b_w_o, c_w_in, c_w_out,
           f_w_up, f_w_conv, f_b_conv, f_w_down, g_final):
    for i in range(DEPTH):
        mod = mods[i]
        kind, j = i % N_MIXERS, i // N_MIXERS
        if kind == 0:
            x = _gmlp_layer(x, mod, norm_g[i, 0], a_w_in[j], a_g_v[j], a_w_s[j], a_b_s[j], a_w_out[j])
        elif kind == 1:
            q, k, v = _qkv_proj(x, mod, norm_g[i, 0], b_w_qkv[j])
            x = _attn_layer(x, mod, q, k, v, b_sinks[j], b_w_o[j])
        else:
            x = _fourier_layer(x, mod, norm_g[i, 0], c_w_in[j], c_w_out[j])
        x = _ffn_layer(x, mod, norm_g[i, 1], f_w_up[i], f_w_conv[i], f_b_conv[i], f_w_down[i], g_final,
                       final=(i == DEPTH - 1))
    return x


def kernel(x_prompt, x_sample, c_prompt, c_sample, w_ada, b_ada, norm_g, a_w_in, a_g_v, a_w_s, a_b_s, a_w_out,
           b_w_qkv, b_sinks, b_w_o, c_w_in, c_w_out, f_w_up, f_w_conv, f_b_conv, f_w_down, g_final):
    nb_p, nb_s = x_prompt.shape[0], x_sample.shape[0]
    assert nb_p + nb_s <= MOD_ROWS
    c_all = jnp.concatenate(
        [c_prompt, c_sample, jnp.zeros((MOD_ROWS - nb_p - nb_s, D_MODEL), F32)], axis=0)
    mods = _modulation(c_all, w_ada, b_ada).reshape(DEPTH, MOD_ROWS, 6, D_MODEL)
    weights = (norm_g, a_w_in, a_g_v, a_w_s, a_b_s, a_w_out, b_w_qkv, b_sinks, b_w_o, c_w_in, c_w_out,
               f_w_up, f_w_conv, f_b_conv, f_w_down, g_final)
    y_prompt = _trunk(x_prompt, mods[:, :nb_p], *weights)
    y_sample = _trunk(x_sample, mods[:, nb_p:nb_p + nb_s], *weights)
    return (y_prompt, y_sample)
```

```python
import functools

import numpy as np
import jax
import jax.numpy as jnp
from jax import lax
from jax.experimental import pallas as pl
from jax.experimental.pallas import tpu as pltpu

F32 = jnp.float32
BF16 = jnp.bfloat16

D_MODEL = 1024
DEPTH = 4
N_MIXERS = 3
EPS = 1e-6
NEG_INF = -1e30
CHUNK = 128
A_GROUPS = 8
A_HEAD = D_MODEL // A_GROUPS
B_HEADS = 16
B_KV_HEADS = 4
B_HEAD_DIM = 64
B_Q_PER_KV = B_HEADS // B_KV_HEADS
WINDOW = 128
BLOCK = 128
Q_WIDTH = B_HEADS * B_HEAD_DIM
KV_WIDTH = B_KV_HEADS * B_HEAD_DIM
C_GROUPS = 8
C_GROUP_DIM = D_MODEL // C_GROUPS
FF_DIM = 2816
FF_CHUNK = 256
HALO = 8

MOD_ROWS = 8
MOD_NT = 1536
VMEM_LIMIT = 56 * 1024 * 1024

TOKEN_TILE = 512


def _params(*sem):
    return pltpu.CompilerParams(dimension_semantics=sem, vmem_limit_bytes=VMEM_LIMIT)


def _norm_mod(x, gain, scale, shift):
    ms = jnp.mean(x * x, axis=-1, keepdims=True)
    return x * lax.rsqrt(ms + EPS) * (gain * (1.0 + scale)) + shift


def _regroup_rows(x, outer):
    rows, cols = x.shape
    return jnp.transpose(x.reshape(outer, rows // outer, cols), (1, 0, 2)).reshape(rows, cols)


def _mod_kernel(c_ref, w_ref, b_ref, o_ref):
    c = c_ref[...]
    cs = c * jax.nn.sigmoid(c)
    o_ref[0] = jnp.dot(cs, w_ref[0], preferred_element_type=F32) + b_ref[0]


def _modulation(c_all, w_ada, b_ada):
    n_out = w_ada.shape[-1]
    return pl.pallas_call(
        _mod_kernel,
        grid=(DEPTH, n_out // MOD_NT),
        in_specs=[
            pl.BlockSpec((MOD_ROWS, D_MODEL), lambda i, j: (0, 0)),
            pl.BlockSpec((1, D_MODEL, MOD_NT), lambda i, j: (i, 0, j)),
            pl.BlockSpec((1, 1, MOD_NT), lambda i, j: (i, 0, j)),
        ],
        out_specs=pl.BlockSpec((1, MOD_ROWS, MOD_NT), lambda i, j: (i, 0, j)),
        out_shape=jax.ShapeDtypeStruct((DEPTH, MOD_ROWS, n_out), F32),
        compiler_params=_params("parallel", "parallel"),
        name="modulation",
    )(c_all, w_ada, b_ada.reshape(DEPTH, 1, n_out))


def _gmlp_kernel(x_ref, mod_ref, g_ref, win_ref, gv_ref, ws_ref, bs_ref, wout_ref, o_ref, gated_ref, *, tile):
    mod = mod_ref[0]
    half = tile // 2
    n_sub = half // CHUNK
    halves = [slice(i * half, (i + 1) * half) for i in range(2)]
    hs = [_norm_mod(x_ref[0, r, :], g_ref[...], mod[1:2], mod[0:1]).astype(BF16) for r in halves]
    uvs = [jnp.dot(h, win_ref[...], preferred_element_type=F32) for h in hs]
    for i, r in enumerate(halves):
        uv = uvs[i]
        uv = 0.5 * uv * (1.0 + lax.erf(uv * np.float32(1.0 / np.sqrt(2.0))))
        u = uv[:, :D_MODEL]
        v = uv[:, D_MODEL:]
        v = v * lax.rsqrt(jnp.mean(v * v, axis=-1, keepdims=True) + EPS) * gv_ref[...]
        vb = v.astype(BF16)
        for g in range(A_GROUPS):
            cols = slice(g * A_HEAD, (g + 1) * A_HEAD)
            rhs = jnp.concatenate([vb[c * CHUNK:(c + 1) * CHUNK, cols] for c in range(n_sub)], axis=1)
            sv = jnp.dot(ws_ref[g], rhs, preferred_element_type=F32)
            for c in range(n_sub):
                rows = slice(c * CHUNK, (c + 1) * CHUNK)
                gated_ref[i, rows, cols] = (u[rows, cols] * (sv[:, c * CHUNK:(c + 1) * CHUNK] + bs_ref[g])).astype(BF16)
        m = jnp.dot(gated_ref[i], wout_ref[...], preferred_element_type=F32)
        o_ref[0, r, :] = x_ref[0, r, :] + mod[2:3] * m


def _gmlp_layer(x, mod, gain, w_in, g_v, w_s, b_s, w_out):
    b, s, d = x.shape
    tile = min(2 * TOKEN_TILE, s)
    bs_full = jnp.broadcast_to(b_s[:, :, None], (A_GROUPS, CHUNK, A_HEAD))
    const2 = lambda bi, i: (0, 0)
    const3 = lambda bi, i: (0, 0, 0)
    return pl.pallas_call(
        functools.partial(_gmlp_kernel, tile=tile),
        grid=(b, s // tile),
        in_specs=[
            pl.BlockSpec((1, tile, d), lambda bi, i: (bi, i, 0)),
            pl.BlockSpec((1, 6, d), lambda bi, i: (bi, 0, 0)),
            pl.BlockSpec((1, d), const2),
            pl.BlockSpec((d, 2 * d), const2),
            pl.BlockSpec((1, d), const2),
            pl.BlockSpec((A_GROUPS, CHUNK, CHUNK), const3),
            pl.BlockSpec((A_GROUPS, CHUNK, A_HEAD), const3),
            pl.BlockSpec((d, d), const2),
        ],
        out_specs=pl.BlockSpec((1, tile, d), lambda bi, i: (bi, i, 0)),
        out_shape=jax.ShapeDtypeStruct(x.shape, F32),
        scratch_shapes=[pltpu.VMEM((2, tile // 2, d), BF16)],
        compiler_params=_params("parallel", "parallel"),
        name="gmlp_mixer",
    )(x, mod, gain.reshape(1, d), w_in.astype(BF16), g_v.reshape(1, d), w_s.astype(BF16), bs_full,
      w_out.astype(BF16))


_PAIRS = B_KV_HEADS // 2
_HEAD_ORDER = [((2 * (t // B_Q_PER_KV) + half) * B_Q_PER_KV + t % B_Q_PER_KV)
               for t in range(B_HEADS // 2) for half in range(2)]
_HEAD_COLS = np.concatenate([np.arange(h * B_HEAD_DIM, (h + 1) * B_HEAD_DIM) for h in _HEAD_ORDER])
GROUP_ROWS = B_Q_PER_KV * BLOCK
BAND = 3 * BLOCK
_NT = (((1,), (1,)), ((), ()))


def _qkv_kernel(x_ref, mod_ref, g_ref, wqk_ref, wvt_ref, q_ref, k_ref, vt_ref, *, tile):
    mod = mod_ref[0]
    h = _norm_mod(x_ref[0], g_ref[...], mod[1:2], mod[0:1]).astype(BF16)
    qk = jnp.dot(h, wqk_ref[...], preferred_element_type=F32)
    for blk in range(tile // BLOCK):
        rows = slice(blk * BLOCK, (blk + 1) * BLOCK)
        for t in range(Q_WIDTH // 128):
            q_ref[0, blk, t * BLOCK:(t + 1) * BLOCK, :] = (
                qk[rows, t * 128:(t + 1) * 128] * (B_HEAD_DIM ** -0.5)).astype(BF16)
    k_ref[0] = qk[:, Q_WIDTH:].astype(BF16)
    vt_ref[0] = lax.dot_general(wvt_ref[...], h, _NT, preferred_element_type=F32).astype(BF16)


def _qkv_proj(x, mod, gain, w_qkv):
    b, s, d = x.shape
    tile = min(TOKEN_TILE, s)
    tok = lambda bi, i: (bi, i, 0)
    const2 = lambda bi, i: (0, 0)
    w_qk = jnp.concatenate([w_qkv[:, _HEAD_COLS], w_qkv[:, Q_WIDTH:Q_WIDTH + KV_WIDTH]], axis=1).astype(BF16)
    w_vt = w_qkv[:, Q_WIDTH + KV_WIDTH:].T.astype(BF16)
    return pl.pallas_call(
        functools.partial(_qkv_kernel, tile=tile),
        grid=(b, s // tile),
        in_specs=[
            pl.BlockSpec((1, tile, d), tok),
            pl.BlockSpec((1, 6, d), lambda bi, i: (bi, 0, 0)),
            pl.BlockSpec((1, d), const2),
            pl.BlockSpec((d, Q_WIDTH + KV_WIDTH), const2),
            pl.BlockSpec((KV_WIDTH, d), const2),
        ],
        out_specs=[
            pl.BlockSpec((1, tile // BLOCK, Q_WIDTH, 128), lambda bi, i: (bi, i, 0, 0)),
            pl.BlockSpec((1, tile, KV_WIDTH), tok),
            pl.BlockSpec((1, KV_WIDTH, tile), lambda bi, i: (bi, 0, i)),
        ],
        out_shape=[
            jax.ShapeDtypeStruct((b, s // BLOCK, Q_WIDTH, 128), BF16),
            jax.ShapeDtypeStruct((b, s, KV_WIDTH), BF16),
            jax.ShapeDtypeStruct((b, KV_WIDTH, s), BF16),
        ],
        compiler_params=_params("parallel", "parallel"),
        name="qkv_proj",
    )(x, mod, gain.reshape(1, d), w_qk, w_vt)


def _attn_bias_table():
    qq = np.arange(BLOCK)[None, :]
    kk = np.arange(BAND)[:, None]
    dist = np.abs(qq + BLOCK - kk)
    slopes = np.exp2(np.float32(-8.0) * np.arange(1, B_HEADS + 1, dtype=np.float32) / np.float32(B_HEADS))
    tbl = np.empty((3, _PAIRS, 2 * BAND, GROUP_ROWS), np.float32)
    for case in range(3):
        valid = dist <= WINDOW
        if case == 0:
            valid = valid & (kk >= BLOCK)
        if case == 2:
            valid = valid & (kk < 2 * BLOCK)
        for p in range(_PAIRS):
            for c in range(B_Q_PER_KV):
                for half in range(2):
                    hd = (2 * p + half) * B_Q_PER_KV + c
                    alibi = (-slopes[hd] * dist.astype(np.float32)).astype(np.float32)
                    tbl[case, p, half * BAND:(half + 1) * BAND, c * BLOCK:(c + 1) * BLOCK] = np.where(
                        valid, alibi, np.float32(NEG_INF))
    return tbl


def _attn_kernel(x_ref, mod_ref, q_ref, kp_ref, k_ref, kn_ref, vp_ref, v_ref, vn_ref, bias_ref, sink_ref, wo_ref,
                 o_ref, kext_ref, vext_ref, ocat_ref, *, tile, nblk):
    i = pl.program_id(1)
    bpt = tile // BLOCK
    kext_ref[0:BLOCK] = kp_ref[0]
    kext_ref[BLOCK:BLOCK + tile] = k_ref[0]
    kext_ref[BLOCK + tile:] = kn_ref[0]
    vext_ref[0] = vp_ref[0]
    for j in range(bpt):
        vext_ref[1 + j] = v_ref[0, :, j * BLOCK:(j + 1) * BLOCK]
    vext_ref[1 + bpt] = vn_ref[0]
    low_lane = lax.broadcasted_iota(jnp.int32, (BAND, 128), 1) < B_HEAD_DIM
    low_row = lax.broadcasted_iota(jnp.int32, (128, BAND), 0) < B_HEAD_DIM
    low_out = lax.broadcasted_iota(jnp.int32, (128, GROUP_ROWS), 0) < B_HEAD_DIM
    ones_row = lax.broadcasted_iota(jnp.int32, (2 * HALO, 2 * BAND), 0) < HALO
    first_half = lax.broadcasted_iota(jnp.int32, (2 * HALO, 2 * BAND), 1) < BAND
    ones = jnp.where(ones_row == first_half, 1.0, 0.0).astype(BF16)

    def scores(qb, p):
        rows = slice(p * 128, (p + 1) * 128)
        n = i * bpt + qb
        case = jnp.where(n == 0, 0, jnp.where(n == nblk - 1, 2, 1))
        kb = kext_ref[qb * BLOCK:qb * BLOCK + BAND, rows]
        zk = jnp.zeros_like(kb)
        kz = jnp.concatenate([jnp.where(low_lane, kb, zk), jnp.where(low_lane, zk, kb)], axis=0)
        q4 = q_ref[0, qb, p * GROUP_ROWS:(p + 1) * GROUP_ROWS, :]
        return lax.dot_general(kz, q4, _NT, preferred_element_type=F32) + bias_ref[case, p]

    def attend(qb, p, sc):
        rows = slice(p * 128, (p + 1) * 128)
        vt = jnp.concatenate([vext_ref[qb, rows, :], vext_ref[qb + 1, rows, :], vext_ref[qb + 2, rows, :]], axis=1)
        zv = jnp.zeros_like(vt)
        vz = jnp.concatenate([
            jnp.concatenate([jnp.where(low_row, vt, zv), jnp.where(low_row, zv, vt)], axis=1), ones], axis=0)
        probs = []
        shift = []
        for half in range(2):
            sh = sc[half * BAND:(half + 1) * BAND]
            sink = sink_ref[p, half]
            mx = jnp.maximum(jnp.max(sh, axis=0, keepdims=True), sink)
            probs.append(jnp.exp(sh - mx).astype(BF16))
            shift.append(jnp.exp(sink - mx))
        pv = jnp.dot(vz, jnp.concatenate(probs, axis=0), preferred_element_type=F32)
        inv_a = 1.0 / (pv[128:129] + shift[0])
        inv_b = 1.0 / (pv[128 + HALO:129 + HALO] + shift[1])
        on = pv[:128] * jnp.where(low_out, inv_a, inv_b)
        for c in range(B_Q_PER_KV):
            t = p * B_Q_PER_KV + c
            ocat_ref[qb * BLOCK:(qb + 1) * BLOCK, t * 128:(t + 1) * 128] = on[:, c * BLOCK:(c + 1) * BLOCK].T

    units = [(qb, p) for qb in range(bpt) for p in range(_PAIRS)]
    sc = scores(*units[0])
    for u, unit in enumerate(units):
        sc_next = scores(*units[u + 1]) if u + 1 < len(units) else None
        attend(*unit, sc)
        sc = sc_next
    m = jnp.dot(ocat_ref[...].astype(BF16), wo_ref[...], preferred_element_type=F32)
    o_ref[0] = x_ref[0] + mod_ref[0][2:3] * m


def _attn_layer(x, mod, q, k, vt, sinks, w_o):
    b, s, d = x.shape
    tile = min(TOKEN_TILE, s)
    bpt = tile // BLOCK
    nblk = s // BLOCK
    assert nblk >= 2
    tok = lambda bi, i: (bi, i, 0)
    prev = lambda bi, i: jnp.maximum(i * bpt - 1, 0)
    nxt = lambda bi, i: jnp.minimum((i + 1) * bpt, nblk - 1)
    bias = jnp.asarray(_attn_bias_table())
    head = np.array([[[(2 * p + half) * B_Q_PER_KV + c for c in range(B_Q_PER_KV)] for half in range(2)]
                     for p in range(_PAIRS)])
    sink_tbl = jnp.repeat(sinks.astype(F32)[head], BLOCK, axis=-1)[:, :, None, :]
    return pl.pallas_call(
        functools.partial(_attn_kernel, tile=tile, nblk=nblk),
        grid=(b, s // tile),
        in_specs=[
            pl.BlockSpec((1, tile, d), tok),
            pl.BlockSpec((1, 6, d), lambda bi, i: (bi, 0, 0)),
            pl.BlockSpec((1, bpt, Q_WIDTH, 128), lambda bi, i: (bi, i, 0, 0)),
            pl.BlockSpec((1, BLOCK, KV_WIDTH), lambda bi, i: (bi, prev(bi, i), 0)),
            pl.BlockSpec((1, tile, KV_WIDTH), tok),
            pl.BlockSpec((1, BLOCK, KV_WIDTH), lambda bi, i: (bi, nxt(bi, i), 0)),
            pl.BlockSpec((1, KV_WIDTH, BLOCK), lambda bi, i: (bi, 0, prev(bi, i))),
            pl.BlockSpec((1, KV_WIDTH, tile), lambda bi, i: (bi, 0, i)),
            pl.BlockSpec((1, KV_WIDTH, BLOCK), lambda bi, i: (bi, 0, nxt(bi, i))),
            pl.BlockSpec(bias.shape, lambda bi, i: (0, 0, 0, 0)),
            pl.BlockSpec(sink_tbl.shape, lambda bi, i: (0, 0, 0, 0)),
            pl.BlockSpec((Q_WIDTH, d), lambda bi, i: (0, 0)),
        ],
        out_specs=pl.BlockSpec((1, tile, d), tok),
        out_shape=jax.ShapeDtypeStruct(x.shape, F32),
        scratch_shapes=[
            pltpu.VMEM((tile + 2 * BLOCK, KV_WIDTH), BF16),
            pltpu.VMEM((bpt + 2, KV_WIDTH, BLOCK), BF16),
            pltpu.VMEM((tile, Q_WIDTH), F32),
        ],
        compiler_params=_params("parallel", "parallel"),
        name="window_attention",
    )(x, mod, q, k, k, k, vt, vt, vt, bias, sink_tbl, w_o[_HEAD_COLS].astype(BF16))


def _dft_tables(seq):
    n1 = BLOCK
    n2 = seq // n1
    k1 = np.arange(n1, dtype=np.int64)
    s1 = np.arange(n1, dtype=np.int64)
    s2 = np.arange(n2, dtype=np.int64)
    pos = s1[None, None, :] * n2 + s2[:, None, None]
    ang = 2.0 * np.pi * ((k1[None, :, None] * pos) % seq).astype(np.float64) / seq
    stage1 = np.concatenate([np.cos(ang), -np.sin(ang)], axis=1).astype(np.float32)
    k2 = np.arange(n2, dtype=np.int64)
    ang2 = 2.0 * np.pi * ((k2[:, None] * s2[None, :]) % n2).astype(np.float64) / n2
    c2, sn2 = np.cos(ang2), np.sin(ang2)
    stage2 = np.block([[c2, sn2], [-sn2, c2]]).astype(np.float32)
    c = np.arange(C_GROUP_DIM, dtype=np.int64)
    angc = 2.0 * np.pi * ((c[:, None] * c[None, :]) % C_GROUP_DIM).astype(np.float64) / C_GROUP_DIM
    chan = (np.concatenate([np.cos(angc), np.sin(angc)], axis=0) / np.sqrt(float(seq) * C_GROUP_DIM)).astype(np.float32)
    return n1, n2, stage1, stage2, chan


def _fourier1_kernel(x_ref, mod_ref, g_ref, win_ref, dft_ref, mid_ref, *, n1, grp):
    mod = mod_ref[0]
    x = x_ref[0].reshape(n1 * grp, D_MODEL)
    h = _norm_mod(x, g_ref[...], mod[1:2], mod[0:1]).astype(BF16)
    z = jnp.dot(h, win_ref[...], preferred_element_type=F32)
    z = _regroup_rows(z, n1).astype(BF16)
    for g in range(grp):
        y = jnp.dot(dft_ref[g], z[g * n1:(g + 1) * n1], preferred_element_type=F32)
        mid_ref[0, 0, g] = y[:n1].astype(BF16)
        mid_ref[0, 1, g] = y[n1:].astype(BF16)


def _fourier2_kernel(x_ref, mod_ref, mid_ref, dft_ref, chan_ref, wout_ref, o_ref, f_ref, *, n2, grp):
    spec = jnp.dot(dft_ref[...], mid_ref[0], preferred_element_type=F32).astype(BF16)
    for kl in range(grp):
        for cg in range(C_GROUPS):
            lo = kl * D_MODEL + cg * C_GROUP_DIM
            lhs = jnp.concatenate([spec[:n2, lo:lo + C_GROUP_DIM], spec[n2:, lo:lo + C_GROUP_DIM]], axis=1)
            f = jnp.dot(lhs, chan_ref[...], preferred_element_type=F32)
            f_ref[kl * n2:(kl + 1) * n2, cg * C_GROUP_DIM:(cg + 1) * C_GROUP_DIM] = f.astype(BF16)
    m = jnp.dot(f_ref[...], wout_ref[...], preferred_element_type=F32)
    m = _regroup_rows(m, grp)
    out = x_ref[0].reshape(n2 * grp, D_MODEL) + mod_ref[0][2:3] * m
    o_ref[0] = out.reshape(n2, grp, D_MODEL)


def _fourier_layer(x, mod, gain, w_in, w_out):
    b, s, d = x.shape
    n1, n2, stage1, stage2, chan = _dft_tables(s)
    grp = HALO
    const2 = lambda bi, j: (0, 0)
    mid = pl.pallas_call(
        functools.partial(_fourier1_kernel, n1=n1, grp=grp),
        grid=(b, n2 // grp),
        in_specs=[
            pl.BlockSpec((1, n1, grp, d), lambda bi, j: (bi, 0, j, 0)),
            pl.BlockSpec((1, 6, d), lambda bi, j: (bi, 0, 0)),
            pl.BlockSpec((1, d), const2),
            pl.BlockSpec((d, d), const2),
            pl.BlockSpec((grp, 2 * n1, n1), lambda bi, j: (j, 0, 0)),
        ],
        out_specs=pl.BlockSpec((1, 2, grp, n1, d), lambda bi, j: (bi, 0, j, 0, 0)),
        out_shape=jax.ShapeDtypeStruct((b, 2, n2, n1, d), BF16),
        compiler_params=_params("parallel", "parallel"),
        name="fourier_stage1",
    )(x.reshape(b, n1, n2, d), mod, gain.reshape(1, d), w_in.astype(BF16), jnp.asarray(stage1).astype(BF16))
    out = pl.pallas_call(
        functools.partial(_fourier2_kernel, n2=n2, grp=grp),
        grid=(b, n1 // grp),
        in_specs=[
            pl.BlockSpec((1, n2, grp, d), lambda bi, j: (bi, 0, j, 0)),
            pl.BlockSpec((1, 6, d), lambda bi, j: (bi, 0, 0)),
            pl.BlockSpec((1, 2 * n2, grp * d), lambda bi, j: (bi, 0, j)),
            pl.BlockSpec((2 * n2, 2 * n2), const2),
            pl.BlockSpec((2 * C_GROUP_DIM, C_GROUP_DIM), const2),
            pl.BlockSpec((d, d), const2),
        ],
        out_specs=pl.BlockSpec((1, n2, grp, d), lambda bi, j: (bi, 0, j, 0)),
        out_shape=jax.ShapeDtypeStruct((b, n2, n1, d), F32),
        scratch_shapes=[pltpu.VMEM((grp * n2, d), BF16)],
        compiler_params=_params("parallel", "parallel"),
        name="fourier_stage2",
    )(x.reshape(b, n2, n1, d), mod, mid.reshape(b, 2 * n2, n1 * d), jnp.asarray(stage2).astype(BF16),
      jnp.asarray(chan).astype(BF16), w_out.astype(BF16))
    return out.reshape(b, s, d)


def _ffn_kernel(xp_ref, x_ref, xn_ref, mod_ref, g_ref, wup_ref, wc_ref, bc_ref, wdn_ref, gf_ref, o_ref,
                h_ref, up_ref, act_ref, *, tile, final):
    i = pl.program_id(1)
    last = pl.num_programs(1) - 1
    mod = mod_ref[0]
    gain = g_ref[...]
    x = _regroup_rows(x_ref[0], HALO)
    h_ref[0:tile] = _norm_mod(x, gain, mod[4:5], mod[3:4]).astype(BF16)
    h_next = _norm_mod(xn_ref[0], gain, mod[4:5], mod[3:4]) * (i < last).astype(F32)
    h_prev = _norm_mod(xp_ref[0], gain, mod[4:5], mod[3:4]) * (i > 0).astype(F32)
    h_ref[tile:] = jnp.concatenate([h_next, h_prev], axis=0).astype(BF16)
    h = h_ref[...]
    n_chunks = FF_DIM // FF_CHUNK
    sub = lax.broadcasted_iota(jnp.int32, (HALO, FF_CHUNK), 0)

    def up_proj(j):
        for half in range(2):
            c0 = half * FF_DIM + j * FF_CHUNK
            up = jnp.dot(h, wup_ref[:, c0:c0 + FF_CHUNK], preferred_element_type=F32)
            dst = up_ref.at[j % 2, half]
            dst[HALO:HALO + tile] = up[:tile]
            dst[0:HALO] = jnp.where(sub == 0, pltpu.roll(up[tile + HALO:], 1, 0),
                                    pltpu.roll(up[tile - HALO:tile], 1, 0))
            dst[HALO + tile:] = jnp.where(sub == HALO - 1, pltpu.roll(up[tile:tile + HALO], HALO - 1, 0),
                                          pltpu.roll(up[0:HALO], HALO - 1, 0))

    def conv(j, half):
        cols = slice(half * FF_DIM + j * FF_CHUNK, half * FF_DIM + (j + 1) * FF_CHUNK)
        w = wc_ref[:, cols]
        src = up_ref.at[j % 2, half]
        return (src[0:tile] * w[0:1] + src[HALO:HALO + tile] * w[1:2] + src[2 * HALO:2 * HALO + tile] * w[2:3]
                + bc_ref[:, cols])

    up_proj(0)
    for j in range(n_chunks):
        if j + 1 < n_chunks:
            up_proj(j + 1)
        a = conv(j, 0)
        g = conv(j, 1)
        act_ref[:, j * FF_CHUNK:(j + 1) * FF_CHUNK] = (a * (g * jax.nn.sigmoid(g))).astype(BF16)
    y = x + mod[5:6] * jnp.dot(act_ref[...], wdn_ref[...], preferred_element_type=F32)
    if final:
        y = y * lax.rsqrt(jnp.mean(y * y, axis=-1, keepdims=True) + EPS) * gf_ref[...]
    o_ref[0] = _regroup_rows(y, tile // HALO)


def _ffn_layer(x, mod, gain, w_up, w_conv, b_conv, w_down, g_final, final):
    b, s, d = x.shape
    tile = min(2 * TOKEN_TILE, s)
    hpt = tile // HALO
    nh = s // HALO
    tok = lambda bi, i: (bi, i, 0)
    const2 = lambda bi, i: (0, 0)
    return pl.pallas_call(
        functools.partial(_ffn_kernel, tile=tile, final=final),
        grid=(b, s // tile),
        in_specs=[
            pl.BlockSpec((1, HALO, d), lambda bi, i: (bi, jnp.maximum(i * hpt - 1, 0), 0)),
            pl.BlockSpec((1, tile, d), tok),
            pl.BlockSpec((1, HALO, d), lambda bi, i: (bi, jnp.minimum((i + 1) * hpt, nh - 1), 0)),
            pl.BlockSpec((1, 6, d), lambda bi, i: (bi, 0, 0)),
            pl.BlockSpec((1, d), const2),
            pl.BlockSpec((d, 2 * FF_DIM), const2),
            pl.BlockSpec((3, 2 * FF_DIM), const2),
            pl.BlockSpec((1, 2 * FF_DIM), const2),
            pl.BlockSpec((FF_DIM, d), const2),
            pl.BlockSpec((1, d), const2),
        ],
        out_specs=pl.BlockSpec((1, tile, d), tok),
        out_shape=jax.ShapeDtypeStruct(x.shape, F32),
        scratch_shapes=[
            pltpu.VMEM((tile + 2 * HALO, d), BF16),
            pltpu.VMEM((2, 2, tile + 2 * HALO, FF_CHUNK), F32),
            pltpu.VMEM((tile, FF_DIM), BF16),
        ],
        compiler_params=_params("parallel", "parallel"),
        name="conv_ffn",
    )(x, x, x, mod, gain.reshape(1, d), w_up.astype(BF16), w_conv, b_conv.reshape(1, 2 * FF_DIM),
      w_down.astype(BF16), g_final.reshape(1, d))


def _trunk(x, mods, norm_g, a_w_in, a_g_v, a_w_s, a_b_s, a_w_out, b_w_qkv, b_sinks, b_w_o, c_w_in, c_w_out,
           f_w_up, f_w_conv, f_b_conv, f_w_down, g_final):
    for i in range(DEPTH):
        mod = mods[i]
        kind, j = i % N_MIXERS, i // N_MIXERS
        if kind == 0:
            x = _gmlp_layer(x, mod, norm_g[i, 0], a_w_in[j], a_g_v[j], a_w_s[j], a_b_s[j], a_w_out[j])
        elif kind == 1:
            q, k, v = _qkv_proj(x, mod, norm_g[i, 0], b_w_qkv[j])
            x = _attn_layer(x, mod, q, k, v, b_sinks[j], b_w_o[j])
        else:
            x = _fourier_layer(x, mod, norm_g[i, 0], c_w_in[j], c_w_out[j])
        x = _ffn_layer(x, mod, norm_g[i, 1], f_w_up[i], f_w_conv[i], f_b_conv[i], f_w_down[i], g_final,
                       final=(i == DEPTH - 1))
    return x


def kernel(x_prompt, x_sample, c_prompt, c_sample, w_ada, b_ada, norm_g, a_w_in, a_g_v, a_w_s, a_b_s, a_w_out,
           b_w_qkv, b_sinks, b_w_o, c_w_in, c_w_out, f_w_up, f_w_conv, f_b_conv, f_w_down, g_final):
    nb_p, nb_s = x_prompt.shape[0], x_sample.shape[0]
    assert nb_p + nb_s <= MOD_ROWS
    c_all = jnp.concatenate(
        [c_prompt, c_sample, jnp.zeros((MOD_ROWS - nb_p - nb_s, D_MODEL), F32)], axis=0)
    mods = _modulation(c_all, w_ada, b_ada).reshape(DEPTH, MOD_ROWS, 6, D_MODEL)
    weights = (norm_g, a_w_in, a_g_v, a_w_s, a_b_s, a_w_out, b_w_qkv, b_sinks, b_w_o, c_w_in, c_w_out,
               f_w_up, f_w_conv, f_b_conv, f_w_down, g_final)
    y_prompt = _trunk(x_prompt, mods[:, :nb_p], *weights)
    y_sample = _trunk(x_sample, mods[:, nb_p:nb_p + nb_s], *weights)
    return (y_prompt, y_sample)
```

```python
import functools

import numpy as np
import jax
import jax.numpy as jnp
from jax import lax
from jax.experimental import pallas as pl
from jax.experimental.pallas import tpu as pltpu

F32 = jnp.float32
BF16 = jnp.bfloat16

D_MODEL = 1024
DEPTH = 4
N_MIXERS = 3
EPS = 1e-6
NEG_INF = -1e30
CHUNK = 128
A_GROUPS = 8
A_HEAD = D_MODEL // A_GROUPS
B_HEADS = 16
B_KV_HEADS = 4
B_HEAD_DIM = 64
B_Q_PER_KV = B_HEADS // B_KV_HEADS
WINDOW = 128
BLOCK = 128
Q_WIDTH = B_HEADS * B_HEAD_DIM
KV_WIDTH = B_KV_HEADS * B_HEAD_DIM
C_GROUPS = 8
C_GROUP_DIM = D_MODEL // C_GROUPS
FF_DIM = 2816
FF_CHUNK = 256
HALO = 8

MOD_ROWS = 8
MOD_NT = 1536
VMEM_LIMIT = 56 * 1024 * 1024

TOKEN_TILE = 512


def _params(*sem):
    return pltpu.CompilerParams(dimension_semantics=sem, vmem_limit_bytes=VMEM_LIMIT)


def _norm_mod(x, gain, scale, shift):
    ms = jnp.mean(x * x, axis=-1, keepdims=True)
    return x * lax.rsqrt(ms + EPS) * (gain * (1.0 + scale)) + shift


def _regroup_rows(x, outer):
    rows, cols = x.shape
    return jnp.transpose(x.reshape(outer, rows // outer, cols), (1, 0, 2)).reshape(rows, cols)


def _mod_kernel(c_ref, w_ref, b_ref, o_ref):
    c = c_ref[...]
    cs = c * jax.nn.sigmoid(c)
    o_ref[0] = jnp.dot(cs, w_ref[0], preferred_element_type=F32) + b_ref[0]


def _modulation(c_all, w_ada, b_ada):
    n_out = w_ada.shape[-1]
    return pl.pallas_call(
        _mod_kernel,
        grid=(DEPTH, n_out // MOD_NT),
        in_specs=[
            pl.BlockSpec((MOD_ROWS, D_MODEL), lambda i, j: (0, 0)),
            pl.BlockSpec((1, D_MODEL, MOD_NT), lambda i, j: (i, 0, j)),
            pl.BlockSpec((1, 1, MOD_NT), lambda i, j: (i, 0, j)),
        ],
        out_specs=pl.BlockSpec((1, MOD_ROWS, MOD_NT), lambda i, j: (i, 0, j)),
        out_shape=jax.ShapeDtypeStruct((DEPTH, MOD_ROWS, n_out), F32),
        compiler_params=_params("parallel", "parallel"),
        name="modulation",
    )(c_all, w_ada, b_ada.reshape(DEPTH, 1, n_out))


def _gmlp_kernel(x_ref, mod_ref, g_ref, win_ref, gv_ref, ws_ref, bs_ref, wout_ref, o_ref, gated_ref, *, tile):
    mod = mod_ref[0]
    half = tile // 2
    n_sub = half // CHUNK
    halves = [slice(i * half, (i + 1) * half) for i in range(2)]
    hs = [_norm_mod(x_ref[0, r, :], g_ref[...], mod[1:2], mod[0:1]).astype(BF16) for r in halves]
    uvs = [jnp.dot(h, win_ref[...], preferred_element_type=F32) for h in hs]
    for i, r in enumerate(halves):
        uv = uvs[i]
        uv = 0.5 * uv * (1.0 + lax.erf(uv * np.float32(1.0 / np.sqrt(2.0))))
        u = uv[:, :D_MODEL]
        v = uv[:, D_MODEL:]
        v = v * lax.rsqrt(jnp.mean(v * v, axis=-1, keepdims=True) + EPS) * gv_ref[...]
        vb = v.astype(BF16)
        for g in range(A_GROUPS):
            cols = slice(g * A_HEAD, (g + 1) * A_HEAD)
            rhs = jnp.concatenate([vb[c * CHUNK:(c + 1) * CHUNK, cols] for c in range(n_sub)], axis=1)
            sv = jnp.dot(ws_ref[g], rhs, preferred_element_type=F32)
            for c in range(n_sub):
                rows = slice(c * CHUNK, (c + 1) * CHUNK)
                gated_ref[i, rows, cols] = (u[rows, cols] * (sv[:, c * CHUNK:(c + 1) * CHUNK] + bs_ref[g])).astype(BF16)
        m = jnp.dot(gated_ref[i], wout_ref[...], preferred_element_type=F32)
        o_ref[0, r, :] = x_ref[0, r, :] + mod[2:3] * m


def _gmlp_layer(x, mod, gain, w_in, g_v, w_s, b_s, w_out):
    b, s, d = x.shape
    tile = min(2 * TOKEN_TILE, s)
    bs_full = jnp.broadcast_to(b_s[:, :, None], (A_GROUPS, CHUNK, A_HEAD))
    const2 = lambda bi, i: (0, 0)
    const3 = lambda bi, i: (0, 0, 0)
    return pl.pallas_call(
        functools.partial(_gmlp_kernel, tile=tile),
        grid=(b, s // tile),
        in_specs=[
            pl.BlockSpec((1, tile, d), lambda bi, i: (bi, i, 0)),
            pl.BlockSpec((1, 6, d), lambda bi, i: (bi, 0, 0)),
            pl.BlockSpec((1, d), const2),
            pl.BlockSpec((d, 2 * d), const2),
            pl.BlockSpec((1, d), const2),
            pl.BlockSpec((A_GROUPS, CHUNK, CHUNK), const3),
            pl.BlockSpec((A_GROUPS, CHUNK, A_HEAD), const3),
            pl.BlockSpec((d, d), const2),
        ],
        out_specs=pl.BlockSpec((1, tile, d), lambda bi, i: (bi, i, 0)),
        out_shape=jax.ShapeDtypeStruct(x.shape, F32),
        scratch_shapes=[pltpu.VMEM((2, tile // 2, d), BF16)],
        compiler_params=_params("parallel", "parallel"),
        name="gmlp_mixer",
    )(x, mod, gain.reshape(1, d), w_in.astype(BF16), g_v.reshape(1, d), w_s.astype(BF16), bs_full,
      w_out.astype(BF16))


_PAIRS = B_KV_HEADS // 2
_HEAD_ORDER = [((2 * (t // B_Q_PER_KV) + half) * B_Q_PER_KV + t % B_Q_PER_KV)
               for t in range(B_HEADS // 2) for half in range(2)]
_HEAD_COLS = np.concatenate([np.arange(h * B_HEAD_DIM, (h + 1) * B_HEAD_DIM) for h in _HEAD_ORDER])
GROUP_ROWS = B_Q_PER_KV * BLOCK
BAND = 3 * BLOCK
_NT = (((1,), (1,)), ((), ()))


def _qkv_kernel(x_ref, mod_ref, g_ref, wqk_ref, wvt_ref, q_ref, k_ref, vt_ref, *, tile):
    mod = mod_ref[0]
    h = _norm_mod(x_ref[0], g_ref[...], mod[1:2], mod[0:1]).astype(BF16)
    qk = jnp.dot(h, wqk_ref[...], preferred_element_type=F32)
    for blk in range(tile // BLOCK):
        rows = slice(blk * BLOCK, (blk + 1) * BLOCK)
        for t in range(Q_WIDTH // 128):
            q_ref[0, blk, t * BLOCK:(t + 1) * BLOCK, :] = (
                qk[rows, t * 128:(t + 1) * 128] * (B_HEAD_DIM ** -0.5)).astype(BF16)
    k_ref[0] = qk[:, Q_WIDTH:].astype(BF16)
    vt_ref[0] = lax.dot_general(wvt_ref[...], h, _NT, preferred_element_type=F32).astype(BF16)


def _qkv_proj(x, mod, gain, w_qkv):
    b, s, d = x.shape
    tile = min(TOKEN_TILE, s)
    tok = lambda bi, i: (bi, i, 0)
    const2 = lambda bi, i: (0, 0)
    w_qk = jnp.concatenate([w_qkv[:, _HEAD_COLS], w_qkv[:, Q_WIDTH:Q_WIDTH + KV_WIDTH]], axis=1).astype(BF16)
    w_vt = w_qkv[:, Q_WIDTH + KV_WIDTH:].T.astype(BF16)
    return pl.pallas_call(
        functools.partial(_qkv_kernel, tile=tile),
        grid=(b, s // tile),
        in_specs=[
            pl.BlockSpec((1, tile, d), tok),
            pl.BlockSpec((1, 6, d), lambda bi, i: (bi, 0, 0)),
            pl.BlockSpec((1, d), const2),
            pl.BlockSpec((d, Q_WIDTH + KV_WIDTH), const2),
            pl.BlockSpec((KV_WIDTH, d), const2),
        ],
        out_specs=[
            pl.BlockSpec((1, tile // BLOCK, Q_WIDTH, 128), lambda bi, i: (bi, i, 0, 0)),
            pl.BlockSpec((1, tile, KV_WIDTH), tok),
            pl.BlockSpec((1, KV_WIDTH, tile), lambda bi, i: (bi, 0, i)),
        ],
        out_shape=[
            jax.ShapeDtypeStruct((b, s // BLOCK, Q_WIDTH, 128), BF16),
            jax.ShapeDtypeStruct((b, s, KV_WIDTH), BF16),
            jax.ShapeDtypeStruct((b, KV_WIDTH, s), BF16),
        ],
        compiler_params=_params("parallel", "parallel"),
        name="qkv_proj",
    )(x, mod, gain.reshape(1, d), w_qk, w_vt)


def _attn_bias_table():
    qq = np.arange(BLOCK)[None, :]
    kk = np.arange(BAND)[:, None]
    dist = np.abs(qq + BLOCK - kk)
    slopes = np.exp2(np.float32(-8.0) * np.arange(1, B_HEADS + 1, dtype=np.float32) / np.float32(B_HEADS))
    tbl = np.empty((3, _PAIRS, 2 * BAND, GROUP_ROWS), np.float32)
    for case in range(3):
        valid = dist <= WINDOW
        if case == 0:
            valid = valid & (kk >= BLOCK)
        if case == 2:
            valid = valid & (kk < 2 * BLOCK)
        for p in range(_PAIRS):
            for c in range(B_Q_PER_KV):
                for half in range(2):
                    hd = (2 * p + half) * B_Q_PER_KV + c
                    alibi = (-slopes[hd] * dist.astype(np.float32)).astype(np.float32)
                    tbl[case, p, half * BAND:(half + 1) * BAND, c * BLOCK:(c + 1) * BLOCK] = np.where(
                        valid, alibi, np.float32(NEG_INF))
    return tbl


def _attn_kernel(x_ref, mod_ref, q_ref, kp_ref, k_ref, kn_ref, vp_ref, v_ref, vn_ref, bias_ref, sink_ref, wo_ref,
                 o_ref, kext_ref, vext_ref, ocat_ref, *, tile, nblk):
    i = pl.program_id(1)
    bpt = tile // BLOCK
    kext_ref[0:BLOCK] = kp_ref[0]
    kext_ref[BLOCK:BLOCK + tile] = k_ref[0]
    kext_ref[BLOCK + tile:] = kn_ref[0]
    vext_ref[0] = vp_ref[0]
    for j in range(bpt):
        vext_ref[1 + j] = v_ref[0, :, j * BLOCK:(j + 1) * BLOCK]
    vext_ref[1 + bpt] = vn_ref[0]
    low_lane = lax.broadcasted_iota(jnp.int32, (BAND, 128), 1) < B_HEAD_DIM
    low_row = lax.broadcasted_iota(jnp.int32, (128, BAND), 0) < B_HEAD_DIM
    low_out = lax.broadcasted_iota(jnp.int32, (128, GROUP_ROWS), 0) < B_HEAD_DIM
    ones_row = lax.broadcasted_iota(jnp.int32, (2 * HALO, 2 * BAND), 0) < HALO
    first_half = lax.broadcasted_iota(jnp.int32, (2 * HALO, 2 * BAND), 1) < BAND
    ones = jnp.where(ones_row == first_half, 1.0, 0.0).astype(BF16)

    def scores(qb, p):
        rows = slice(p * 128, (p + 1) * 128)
        n = i * bpt + qb
        case = jnp.where(n == 0, 0, jnp.where(n == nblk - 1, 2, 1))
        kb = kext_ref[qb * BLOCK:qb * BLOCK + BAND, rows]
        zk = jnp.zeros_like(kb)
        kz = jnp.concatenate([jnp.where(low_lane, kb, zk), jnp.where(low_lane, zk, kb)], axis=0)
        q4 = q_ref[0, qb, p * GROUP_ROWS:(p + 1) * GROUP_ROWS, :]
        return lax.dot_general(kz, q4, _NT, preferred_element_type=F32) + bias_ref[case, p]

    def attend(qb, p, sc):
        rows = slice(p * 128, (p + 1) * 128)
        vt = jnp.concatenate([vext_ref[qb, rows, :], vext_ref[qb + 1, rows, :], vext_ref[qb + 2, rows, :]], axis=1)
        zv = jnp.zeros_like(vt)
        vz = jnp.concatenate([
            jnp.concatenate([jnp.where(low_row, vt, zv), jnp.where(low_row, zv, vt)], axis=1), ones], axis=0)
        probs = []
        shift = []
        for half in range(2):
            sh = sc[half * BAND:(half + 1) * BAND]
            sink = sink_ref[p, half]
            mx = jnp.maximum(jnp.max(sh, axis=0, keepdims=True), sink)
            probs.append(jnp.exp(sh - mx).astype(BF16))
            shift.append(jnp.exp(sink - mx))
        pv = jnp.dot(vz, jnp.concatenate(probs, axis=0), preferred_element_type=F32)
        inv_a = 1.0 / (pv[128:129] + shift[0])
        inv_b = 1.0 / (pv[128 + HALO:129 + HALO] + shift[1])
        on = pv[:128] * jnp.where(low_out, inv_a, inv_b)
        for c in range(B_Q_PER_KV):
            t = p * B_Q_PER_KV + c
            ocat_ref[qb * BLOCK:(qb + 1) * BLOCK, t * 128:(t + 1) * 128] = on[:, c * BLOCK:(c + 1) * BLOCK].T

    units = [(qb, p) for qb in range(bpt) for p in range(_PAIRS)]
    sc = scores(*units[0])
    for u, unit in enumerate(units):
        sc_next = scores(*units[u + 1]) if u + 1 < len(units) else None
        attend(*unit, sc)
        sc = sc_next
    m = jnp.dot(ocat_ref[...].astype(BF16), wo_ref[...], preferred_element_type=F32)
    o_ref[0] = x_ref[0] + mod_ref[0][2:3] * m


def _attn_layer(x, mod, q, k, vt, sinks, w_o):
    b, s, d = x.shape
    tile = min(TOKEN_TILE, s)
    bpt = tile // BLOCK
    nblk = s // BLOCK
    assert nblk >= 2
    tok = lambda bi, i: (bi, i, 0)
    prev = lambda bi, i: jnp.maximum(i * bpt - 1, 0)
    nxt = lambda bi, i: jnp.minimum((i + 1) * bpt, nblk - 1)
    bias = jnp.asarray(_attn_bias_table())
    head = np.array([[[(2 * p + half) * B_Q_PER_KV + c for c in range(B_Q_PER_KV)] for half in range(2)]
                     for p in range(_PAIRS)])
    sink_tbl = jnp.repeat(sinks.astype(F32)[head], BLOCK, axis=-1)[:, :, None, :]
    return pl.pallas_call(
        functools.partial(_attn_kernel, tile=tile, nblk=nblk),
        grid=(b, s // tile),
        in_specs=[
            pl.BlockSpec((1, tile, d), tok),
            pl.BlockSpec((1, 6, d), lambda bi, i: (bi, 0, 0)),
            pl.BlockSpec((1, bpt, Q_WIDTH, 128), lambda bi, i: (bi, i, 0, 0)),
            pl.BlockSpec((1, BLOCK, KV_WIDTH), lambda bi, i: (bi, prev(bi, i), 0)),
            pl.BlockSpec((1, tile, KV_WIDTH), tok),
            pl.BlockSpec((1, BLOCK, KV_WIDTH), lambda bi, i: (bi, nxt(bi, i), 0)),
            pl.BlockSpec((1, KV_WIDTH, BLOCK), lambda bi, i: (bi, 0, prev(bi, i))),
            pl.BlockSpec((1, KV_WIDTH, tile), lambda bi, i: (bi, 0, i)),
            pl.BlockSpec((1, KV_WIDTH, BLOCK), lambda bi, i: (bi, 0, nxt(bi, i))),
            pl.BlockSpec(bias.shape, lambda bi, i: (0, 0, 0, 0)),
            pl.BlockSpec(sink_tbl.shape, lambda bi, i: (0, 0, 0, 0)),
            pl.BlockSpec((Q_WIDTH, d), lambda bi, i: (0, 0)),
        ],
        out_specs=pl.BlockSpec((1, tile, d), tok),
        out_shape=jax.ShapeDtypeStruct(x.shape, F32),
        scratch_shapes=[
            pltpu.VMEM((tile + 2 * BLOCK, KV_WIDTH), BF16),
            pltpu.VMEM((bpt + 2, KV_WIDTH, BLOCK), BF16),
            pltpu.VMEM((tile, Q_WIDTH), F32),
        ],
        compiler_params=_params("parallel", "parallel"),
        name="window_attention",
    )(x, mod, q, k, k, k, vt, vt, vt, bias, sink_tbl, w_o[_HEAD_COLS].astype(BF16))


def _dft_tables(seq):
    n1 = BLOCK
    n2 = seq // n1
    k1 = np.arange(n1, dtype=np.int64)
    s1 = np.arange(n1, dtype=np.int64)
    s2 = np.arange(n2, dtype=np.int64)
    pos = s1[None, None, :] * n2 + s2[:, None, None]
    ang = 2.0 * np.pi * ((k1[None, :, None] * pos) % seq).astype(np.float64) / seq
    stage1 = np.concatenate([np.cos(ang), -np.sin(ang)], axis=1).astype(np.float32)
    k2 = np.arange(n2, dtype=np.int64)
    ang2 = 2.0 * np.pi * ((k2[:, None] * s2[None, :]) % n2).astype(np.float64) / n2
    c2, sn2 = np.cos(ang2), np.sin(ang2)
    stage2 = np.block([[c2, sn2], [-sn2, c2]]).astype(np.float32)
    stage2 = stage2.reshape(2 * n2, 2, n2).transpose(0, 2, 1).reshape(2 * n2, 2 * n2)
    c = np.arange(C_GROUP_DIM, dtype=np.int64)
    angc = 2.0 * np.pi * ((c[:, None] * c[None, :]) % C_GROUP_DIM).astype(np.float64) / C_GROUP_DIM
    chan = (np.concatenate([np.cos(angc), np.sin(angc)], axis=0) / np.sqrt(float(seq) * C_GROUP_DIM)).astype(np.float32)
    return n1, n2, stage1, stage2, chan


def _fourier1_kernel(x_ref, mod_ref, g_ref, win_ref, dft_ref, mid_ref, *, n1, grp):
    mod = mod_ref[0]
    x = x_ref[0].reshape(n1 * grp, D_MODEL)
    h = _norm_mod(x, g_ref[...], mod[1:2], mod[0:1]).astype(BF16)
    z = jnp.dot(h, win_ref[...], preferred_element_type=F32)
    z = _regroup_rows(z, n1).astype(BF16)
    y = jnp.concatenate([jnp.dot(dft_ref[g], z[g * n1:(g + 1) * n1], preferred_element_type=F32)
                         for g in range(grp)], axis=0)
    mid_ref[0] = _regroup_rows(y, 2 * grp).astype(BF16).reshape(n1, 2 * grp, D_MODEL)


def _fourier2_kernel(x_ref, mod_ref, mid_ref, dft_ref, chan_ref, wout_ref, o_ref, f_ref, *, n2, grp):
    for kl in range(grp):
        spec = jnp.dot(dft_ref[...], mid_ref[0, kl], preferred_element_type=F32).astype(BF16)
        for cg in range(C_GROUPS):
            cols = slice(cg * C_GROUP_DIM, (cg + 1) * C_GROUP_DIM)
            lhs = jnp.concatenate([spec[:n2, cols], spec[n2:, cols]], axis=1)
            f = jnp.dot(lhs, chan_ref[...], preferred_element_type=F32)
            f_ref[kl * n2:(kl + 1) * n2, cols] = f.astype(BF16)
    m = jnp.dot(f_ref[...], wout_ref[...], preferred_element_type=F32)
    m = _regroup_rows(m, grp)
    out = x_ref[0].reshape(n2 * grp, D_MODEL) + mod_ref[0][2:3] * m
    o_ref[0] = out.reshape(n2, grp, D_MODEL)


def _fourier_layer(x, mod, gain, w_in, w_out):
    b, s, d = x.shape
    n1, n2, stage1, stage2, chan = _dft_tables(s)
    grp = HALO
    const2 = lambda bi, j: (0, 0)
    mid = pl.pallas_call(
        functools.partial(_fourier1_kernel, n1=n1, grp=grp),
        grid=(b, n2 // grp),
        in_specs=[
            pl.BlockSpec((1, n1, grp, d), lambda bi, j: (bi, 0, j, 0)),
            pl.BlockSpec((1, 6, d), lambda bi, j: (bi, 0, 0)),
            pl.BlockSpec((1, d), const2),
            pl.BlockSpec((d, d), const2),
            pl.BlockSpec((grp, 2 * n1, n1), lambda bi, j: (j, 0, 0)),
        ],
        out_specs=pl.BlockSpec((1, n1, 2 * grp, d), lambda bi, j: (bi, 0, j, 0)),
        out_shape=jax.ShapeDtypeStruct((b, n1, 2 * n2, d), BF16),
        compiler_params=_params("parallel", "parallel"),
        name="fourier_stage1",
    )(x.reshape(b, n1, n2, d), mod, gain.reshape(1, d), w_in.astype(BF16), jnp.asarray(stage1).astype(BF16))
    out = pl.pallas_call(
        functools.partial(_fourier2_kernel, n2=n2, grp=grp),
        grid=(b, n1 // grp),
        in_specs=[
            pl.BlockSpec((1, n2, grp, d), lambda bi, j: (bi, 0, j, 0)),
            pl.BlockSpec((1, 6, d), lambda bi, j: (bi, 0, 0)),
            pl.BlockSpec((1, grp, 2 * n2, d), lambda bi, j: (bi, j, 0, 0)),
            pl.BlockSpec((2 * n2, 2 * n2), const2),
            pl.BlockSpec((2 * C_GROUP_DIM, C_GROUP_DIM), const2),
            pl.BlockSpec((d, d), const2),
        ],
        out_specs=pl.BlockSpec((1, n2, grp, d), lambda bi, j: (bi, 0, j, 0)),
        out_shape=jax.ShapeDtypeStruct((b, n2, n1, d), F32),
        scratch_shapes=[pltpu.VMEM((grp * n2, d), BF16)],
        compiler_params=_params("parallel", "parallel"),
        name="fourier_stage2",
    )(x.reshape(b, n2, n1, d), mod, mid, jnp.asarray(stage2).astype(BF16),
      jnp.asarray(chan).astype(BF16), w_out.astype(BF16))
    return out.reshape(b, s, d)


def _ffn_kernel(xp_ref, x_ref, xn_ref, mod_ref, g_ref, wup_ref, wc_ref, bc_ref, wdn_ref, gf_ref, o_ref,
                h_ref, up_ref, act_ref, *, tile, final):
    i = pl.program_id(1)
    last = pl.num_programs(1) - 1
    mod = mod_ref[0]
    gain = g_ref[...]
    x = _regroup_rows(x_ref[0], HALO)
    h_ref[0:tile] = _norm_mod(x, gain, mod[4:5], mod[3:4]).astype(BF16)
    h_next = _norm_mod(xn_ref[0], gain, mod[4:5], mod[3:4]) * (i < last).astype(F32)
    h_prev = _norm_mod(xp_ref[0], gain, mod[4:5], mod[3:4]) * (i > 0).astype(F32)
    h_ref[tile:] = jnp.concatenate([h_next, h_prev], axis=0).astype(BF16)
    h = h_ref[...]
    n_chunks = FF_DIM // FF_CHUNK
    sub = lax.broadcasted_iota(jnp.int32, (HALO, FF_CHUNK), 0)

    def up_proj(j):
        for half in range(2):
            c0 = half * FF_DIM + j * FF_CHUNK
            up = jnp.dot(h, wup_ref[:, c0:c0 + FF_CHUNK], preferred_element_type=F32)
            dst = up_ref.at[j % 2, half]
            dst[HALO:HALO + tile] = up[:tile]
            dst[0:HALO] = jnp.where(sub == 0, pltpu.roll(up[tile + HALO:], 1, 0),
                                    pltpu.roll(up[tile - HALO:tile], 1, 0))
            dst[HALO + tile:] = jnp.where(sub == HALO - 1, pltpu.roll(up[tile:tile + HALO], HALO - 1, 0),
                                          pltpu.roll(up[0:HALO], HALO - 1, 0))

    def conv(j, half):
        cols = slice(half * FF_DIM + j * FF_CHUNK, half * FF_DIM + (j + 1) * FF_CHUNK)
        w = wc_ref[:, cols]
        src = up_ref.at[j % 2, half]
        return (src[0:tile] * w[0:1] + src[HALO:HALO + tile] * w[1:2] + src[2 * HALO:2 * HALO + tile] * w[2:3]
                + bc_ref[:, cols])

    up_proj(0)
    for j in range(n_chunks):
        if j + 1 < n_chunks:
            up_proj(j + 1)
        a = conv(j, 0)
        g = conv(j, 1)
        act_ref[:, j * FF_CHUNK:(j + 1) * FF_CHUNK] = (a * (g * jax.nn.sigmoid(g))).astype(BF16)
    y = x + mod[5:6] * jnp.dot(act_ref[...], wdn_ref[...], preferred_element_type=F32)
    if final:
        y = y * lax.rsqrt(jnp.mean(y * y, axis=-1, keepdims=True) + EPS) * gf_ref[...]
    o_ref[0] = _regroup_rows(y, tile // HALO)


def _ffn_layer(x, mod, gain, w_up, w_conv, b_conv, w_down, g_final, final):
    b, s, d = x.shape
    tile = min(2 * TOKEN_TILE, s)
    hpt = tile // HALO
    nh = s // HALO
    tok = lambda bi, i: (bi, i, 0)
    const2 = lambda bi, i: (0, 0)
    return pl.pallas_call(
        functools.partial(_ffn_kernel, tile=tile, final=final),
        grid=(b, s // tile),
        in_specs=[
            pl.BlockSpec((1, HALO, d), lambda bi, i: (bi, jnp.maximum(i * hpt - 1, 0), 0)),
            pl.BlockSpec((1, tile, d), tok),
            pl.BlockSpec((1, HALO, d), lambda bi, i: (bi, jnp.minimum((i + 1) * hpt, nh - 1), 0)),
            pl.BlockSpec((1, 6, d), lambda bi, i: (bi, 0, 0)),
            pl.BlockSpec((1, d), const2),
            pl.BlockSpec((d, 2 * FF_DIM), const2),
            pl.BlockSpec((3, 2 * FF_DIM), const2),
            pl.BlockSpec((1, 2 * FF_DIM), const2),
            pl.BlockSpec((FF_DIM, d), const2),
            pl.BlockSpec((1, d), const2),
        ],
        out_specs=pl.BlockSpec((1, tile, d), tok),
        out_shape=jax.ShapeDtypeStruct(x.shape, F32),
        scratch_shapes=[
            pltpu.VMEM((tile + 2 * HALO, d), BF16),
            pltpu.VMEM((2, 2, tile + 2 * HALO, FF_CHUNK), F32),
            pltpu.VMEM((tile, FF_DIM), BF16),
        ],
        compiler_params=_params("parallel", "parallel"),
        name="conv_ffn",
    )(x, x, x, mod, gain.reshape(1, d), w_up.astype(BF16), w_conv, b_conv.reshape(1, 2 * FF_DIM),
      w_down.astype(BF16), g_final.reshape(1, d))


def _trunk(x, mods, norm_g, a_w_in, a_g_v, a_w_s, a_b_s, a_w_out, b_w_qkv, b_sinks, b_w_o, c_w_in, c_w_out,
           f_w_up, f_w_conv, f_b_conv, f_w_down, g_final):
    for i in range(DEPTH):
        mod = mods[i]
        kind, j = i % N_MIXERS, i // N_MIXERS
        if kind == 0:
            x = _gmlp_layer(x, mod, norm_g[i, 0], a_w_in[j], a_g_v[j], a_w_s[j], a_b_s[j], a_w_out[j])
        elif kind == 1:
            q, k, v = _qkv_proj(x, mod, norm_g[i, 0], b_w_qkv[j])
            x = _attn_layer(x, mod, q, k, v, b_sinks[j], b_w_o[j])
        else:
            x = _fourier_layer(x, mod, norm_g[i, 0], c_w_in[j], c_w_out[j])
        x = _ffn_layer(x, mod, norm_g[i, 1], f_w_up[i], f_w_conv[i], f_b_conv[i], f_w_down[i], g_final,
                       final=(i == DEPTH - 1))
    return x


def kernel(x_prompt, x_sample, c_prompt, c_sample, w_ada, b_ada, norm_g, a_w_in, a_g_v, a_w_s, a_b_s, a_w_out,
           b_w_qkv, b_sinks, b_w_o, c_w_in, c_w_out, f_w_up, f_w_conv, f_b_conv, f_w_down, g_final):
    nb_p, nb_s = x_prompt.shape[0], x_sample.shape[0]
    assert nb_p + nb_s <= MOD_ROWS
    c_all = jnp.concatenate(
        [c_prompt, c_sample, jnp.zeros((MOD_ROWS - nb_p - nb_s, D_MODEL), F32)], axis=0)
    mods = _modulation(c_all, w_ada, b_ada).reshape(DEPTH, MOD_ROWS, 6, D_MODEL)
    weights = (norm_g, a_w_in, a_g_v, a_w_s, a_b_s, a_w_out, b_w_qkv, b_sinks, b_w_o, c_w_in, c_w_out,
               f_w_up, f_w_conv, f_b_conv, f_w_down, g_final)
    y_prompt = _trunk(x_prompt, mods[:, :nb_p], *weights)
    y_sample = _trunk(x_sample, mods[:, nb_p:nb_p + nb_s], *weights)
    return (y_prompt, y_sample)
```

```python
import functools

import numpy as np
import jax
import jax.numpy as jnp
from jax import lax
from jax.experimental import pallas as pl
from jax.experimental.pallas import tpu as pltpu

F32 = jnp.float32
BF16 = jnp.bfloat16

D_MODEL = 1024
DEPTH = 4
N_MIXERS = 3
EPS = 1e-6
NEG_INF = -1e30
CHUNK = 128
A_GROUPS = 8
A_HEAD = D_MODEL // A_GROUPS
B_HEADS = 16
B_KV_HEADS = 4
B_HEAD_DIM = 64
B_Q_PER_KV = B_HEADS // B_KV_HEADS
WINDOW = 128
BLOCK = 128
Q_WIDTH = B_HEADS * B_HEAD_DIM
KV_WIDTH = B_KV_HEADS * B_HEAD_DIM
C_GROUPS = 8
C_GROUP_DIM = D_MODEL // C_GROUPS
FF_DIM = 2816
FF_CHUNK = 256
HALO = 8

MOD_ROWS = 8
MOD_NT = 1536
VMEM_LIMIT = 56 * 1024 * 1024

TOKEN_TILE = 512


def _params(*sem):
    return pltpu.CompilerParams(dimension_semantics=sem, vmem_limit_bytes=VMEM_LIMIT)


def _norm_mod(x, gain, scale, shift):
    ms = jnp.mean(x * x, axis=-1, keepdims=True)
    return x * lax.rsqrt(ms + EPS) * (gain * (1.0 + scale)) + shift


def _regroup_rows(x, outer):
    rows, cols = x.shape
    return jnp.transpose(x.reshape(outer, rows // outer, cols), (1, 0, 2)).reshape(rows, cols)


def _mod_kernel(c_ref, w_ref, b_ref, o_ref):
    c = c_ref[...]
    cs = c * jax.nn.sigmoid(c)
    o_ref[0] = jnp.dot(cs, w_ref[0], preferred_element_type=F32) + b_ref[0]


def _modulation(c_all, w_ada, b_ada):
    n_out = w_ada.shape[-1]
    return pl.pallas_call(
        _mod_kernel,
        grid=(DEPTH, n_out // MOD_NT),
        in_specs=[
            pl.BlockSpec((MOD_ROWS, D_MODEL), lambda i, j: (0, 0)),
            pl.BlockSpec((1, D_MODEL, MOD_NT), lambda i, j: (i, 0, j)),
            pl.BlockSpec((1, 1, MOD_NT), lambda i, j: (i, 0, j)),
        ],
        out_specs=pl.BlockSpec((1, MOD_ROWS, MOD_NT), lambda i, j: (i, 0, j)),
        out_shape=jax.ShapeDtypeStruct((DEPTH, MOD_ROWS, n_out), F32),
        compiler_params=_params("parallel", "parallel"),
        name="modulation",
    )(c_all, w_ada, b_ada.reshape(DEPTH, 1, n_out))


def _gmlp_kernel(x_ref, mod_ref, g_ref, win_ref, gv_ref, ws_ref, bs_ref, wout_ref, o_ref, gated_ref, *, tile):
    mod = mod_ref[0]
    half = tile // 2
    n_sub = half // CHUNK
    halves = [slice(i * half, (i + 1) * half) for i in range(2)]
    hs = [_norm_mod(x_ref[0, r, :], g_ref[...], mod[1:2], mod[0:1]).astype(BF16) for r in halves]
    uvs = [jnp.dot(h, win_ref[...], preferred_element_type=F32) for h in hs]
    for i, r in enumerate(halves):
        uv = uvs[i]
        uv = 0.5 * uv * (1.0 + lax.erf(uv * np.float32(1.0 / np.sqrt(2.0))))
        u = uv[:, :D_MODEL]
        v = uv[:, D_MODEL:]
        v = v * lax.rsqrt(jnp.mean(v * v, axis=-1, keepdims=True) + EPS) * gv_ref[...]
        vb = v.astype(BF16)
        for g in range(A_GROUPS):
            cols = slice(g * A_HEAD, (g + 1) * A_HEAD)
            rhs = jnp.concatenate([vb[c * CHUNK:(c + 1) * CHUNK, cols] for c in range(n_sub)], axis=1)
            sv = jnp.dot(ws_ref[g], rhs, preferred_element_type=F32)
            for c in range(n_sub):
                rows = slice(c * CHUNK, (c + 1) * CHUNK)
                gated_ref[i, rows, cols] = (u[rows, cols] * (sv[:, c * CHUNK:(c + 1) * CHUNK] + bs_ref[g])).astype(BF16)
        m = jnp.dot(gated_ref[i], wout_ref[...], preferred_element_type=F32)
        o_ref[0, r, :] = x_ref[0, r, :] + mod[2:3] * m


def _gmlp_layer(x, mod, gain, w_in, g_v, w_s, b_s, w_out):
    b, s, d = x.shape
    tile = min(2 * TOKEN_TILE, s)
    bs_full = jnp.broadcast_to(b_s[:, :, None], (A_GROUPS, CHUNK, A_HEAD))
    const2 = lambda bi, i: (0, 0)
    const3 = lambda bi, i: (0, 0, 0)
    return pl.pallas_call(
        functools.partial(_gmlp_kernel, tile=tile),
        grid=(b, s // tile),
        in_specs=[
            pl.BlockSpec((1, tile, d), lambda bi, i: (bi, i, 0)),
            pl.BlockSpec((1, 6, d), lambda bi, i: (bi, 0, 0)),
            pl.BlockSpec((1, d), const2),
            pl.BlockSpec((d, 2 * d), const2),
            pl.BlockSpec((1, d), const2),
            pl.BlockSpec((A_GROUPS, CHUNK, CHUNK), const3),
            pl.BlockSpec((A_GROUPS, CHUNK, A_HEAD), const3),
            pl.BlockSpec((d, d), const2),
        ],
        out_specs=pl.BlockSpec((1, tile, d), lambda bi, i: (bi, i, 0)),
        out_shape=jax.ShapeDtypeStruct(x.shape, F32),
        scratch_shapes=[pltpu.VMEM((2, tile // 2, d), BF16)],
        compiler_params=_params("parallel", "parallel"),
        name="gmlp_mixer",
    )(x, mod, gain.reshape(1, d), w_in.astype(BF16), g_v.reshape(1, d), w_s.astype(BF16), bs_full,
      w_out.astype(BF16))


_PAIRS = B_KV_HEADS // 2
_HEAD_ORDER = [((2 * (t // B_Q_PER_KV) + half) * B_Q_PER_KV + t % B_Q_PER_KV)
               for t in range(B_HEADS // 2) for half in range(2)]
_HEAD_COLS = np.concatenate([np.arange(h * B_HEAD_DIM, (h + 1) * B_HEAD_DIM) for h in _HEAD_ORDER])
GROUP_ROWS = B_Q_PER_KV * BLOCK
BAND = 3 * BLOCK
LOG2E = float(np.log2(np.e))
_NT = (((1,), (1,)), ((), ()))


def _qkv_kernel(x_ref, mod_ref, g_ref, wqk_ref, wvt_ref, q_ref, k_ref, vt_ref, *, tile):
    mod = mod_ref[0]
    h = _norm_mod(x_ref[0], g_ref[...], mod[1:2], mod[0:1]).astype(BF16)
    qk = jnp.dot(h, wqk_ref[...], preferred_element_type=F32)
    for blk in range(tile // BLOCK):
        rows = slice(blk * BLOCK, (blk + 1) * BLOCK)
        for t in range(Q_WIDTH // 128):
            q_ref[0, blk, t * BLOCK:(t + 1) * BLOCK, :] = (
                qk[rows, t * 128:(t + 1) * 128] * (B_HEAD_DIM ** -0.5 * LOG2E)).astype(BF16)
    k_ref[0] = qk[:, Q_WIDTH:].astype(BF16)
    vt_ref[0] = lax.dot_general(wvt_ref[...], h, _NT, preferred_element_type=F32).astype(BF16)


def _qkv_proj(x, mod, gain, w_qkv):
    b, s, d = x.shape
    tile = min(TOKEN_TILE, s)
    tok = lambda bi, i: (bi, i, 0)
    const2 = lambda bi, i: (0, 0)
    w_qk = jnp.concatenate([w_qkv[:, _HEAD_COLS], w_qkv[:, Q_WIDTH:Q_WIDTH + KV_WIDTH]], axis=1).astype(BF16)
    w_vt = w_qkv[:, Q_WIDTH + KV_WIDTH:].T.astype(BF16)
    return pl.pallas_call(
        functools.partial(_qkv_kernel, tile=tile),
        grid=(b, s // tile),
        in_specs=[
            pl.BlockSpec((1, tile, d), tok),
            pl.BlockSpec((1, 6, d), lambda bi, i: (bi, 0, 0)),
            pl.BlockSpec((1, d), const2),
            pl.BlockSpec((d, Q_WIDTH + KV_WIDTH), const2),
            pl.BlockSpec((KV_WIDTH, d), const2),
        ],
        out_specs=[
            pl.BlockSpec((1, tile // BLOCK, Q_WIDTH, 128), lambda bi, i: (bi, i, 0, 0)),
            pl.BlockSpec((1, tile, KV_WIDTH), tok),
            pl.BlockSpec((1, KV_WIDTH, tile), lambda bi, i: (bi, 0, i)),
        ],
        out_shape=[
            jax.ShapeDtypeStruct((b, s // BLOCK, Q_WIDTH, 128), BF16),
            jax.ShapeDtypeStruct((b, s, KV_WIDTH), BF16),
            jax.ShapeDtypeStruct((b, KV_WIDTH, s), BF16),
        ],
        compiler_params=_params("parallel", "parallel"),
        name="qkv_proj",
    )(x, mod, gain.reshape(1, d), w_qk, w_vt)


def _attn_bias_table():
    qq = np.arange(BLOCK)[None, :]
    kk = np.arange(BAND)[:, None]
    dist = np.abs(qq + BLOCK - kk)
    slopes = np.exp2(np.float32(-8.0) * np.arange(1, B_HEADS + 1, dtype=np.float32) / np.float32(B_HEADS))
    tbl = np.empty((3, _PAIRS, 2 * BAND, GROUP_ROWS), np.float32)
    for case in range(3):
        valid = dist <= WINDOW
        if case == 0:
            valid = valid & (kk >= BLOCK)
        if case == 2:
            valid = valid & (kk < 2 * BLOCK)
        for p in range(_PAIRS):
            for c in range(B_Q_PER_KV):
                for half in range(2):
                    hd = (2 * p + half) * B_Q_PER_KV + c
                    alibi = (-slopes[hd].astype(np.float64) * dist * LOG2E).astype(np.float32)
                    tbl[case, p, half * BAND:(half + 1) * BAND, c * BLOCK:(c + 1) * BLOCK] = np.where(
                        valid, alibi, np.float32(NEG_INF))
    return tbl


def _attn_kernel(x_ref, mod_ref, q_ref, kp_ref, k_ref, kn_ref, vp_ref, v_ref, vn_ref, bias_ref, sink_ref, wo_ref,
                 o_ref, kext_ref, vext_ref, ocat_ref, *, tile, nblk):
    i = pl.program_id(1)
    bpt = tile // BLOCK
    kext_ref[0:BLOCK] = kp_ref[0]
    kext_ref[BLOCK:BLOCK + tile] = k_ref[0]
    kext_ref[BLOCK + tile:] = kn_ref[0]
    vext_ref[0] = vp_ref[0]
    for j in range(bpt):
        vext_ref[1 + j] = v_ref[0, :, j * BLOCK:(j + 1) * BLOCK]
    vext_ref[1 + bpt] = vn_ref[0]
    low_lane = lax.broadcasted_iota(jnp.int32, (BAND, 128), 1) < B_HEAD_DIM
    low_row = lax.broadcasted_iota(jnp.int32, (128, BAND), 0) < B_HEAD_DIM
    low_out = lax.broadcasted_iota(jnp.int32, (128, GROUP_ROWS), 0) < B_HEAD_DIM
    ones_row = lax.broadcasted_iota(jnp.int32, (2 * HALO, 2 * BAND), 0) < HALO
    first_half = lax.broadcasted_iota(jnp.int32, (2 * HALO, 2 * BAND), 1) < BAND
    ones = jnp.where(ones_row == first_half, 1.0, 0.0).astype(BF16)

    def scores(qb, p):
        rows = slice(p * 128, (p + 1) * 128)
        n = i * bpt + qb
        case = jnp.where(n == 0, 0, jnp.where(n == nblk - 1, 2, 1))
        kb = kext_ref[qb * BLOCK:qb * BLOCK + BAND, rows]
        zk = jnp.zeros_like(kb)
        kz = jnp.concatenate([jnp.where(low_lane, kb, zk), jnp.where(low_lane, zk, kb)], axis=0)
        q4 = q_ref[0, qb, p * GROUP_ROWS:(p + 1) * GROUP_ROWS, :]
        return lax.dot_general(kz, q4, _NT, preferred_element_type=F32) + bias_ref[case, p]

    def attend(qb, p, sc):
        rows = slice(p * 128, (p + 1) * 128)
        vt = jnp.concatenate([vext_ref[qb, rows, :], vext_ref[qb + 1, rows, :], vext_ref[qb + 2, rows, :]], axis=1)
        zv = jnp.zeros_like(vt)
        vz = jnp.concatenate([
            jnp.concatenate([jnp.where(low_row, vt, zv), jnp.where(low_row, zv, vt)], axis=1), ones], axis=0)
        probs = []
        shift = []
        for half in range(2):
            sh = sc[half * BAND:(half + 1) * BAND]
            sink = sink_ref[p, half]
            mx = jnp.maximum(jnp.max(sh, axis=0, keepdims=True), sink)
            probs.append(jnp.exp2(sh - mx).astype(BF16))
            shift.append(jnp.exp2(sink - mx))
        pv = jnp.dot(vz, jnp.concatenate(probs, axis=0), preferred_element_type=F32)
        inv_a = 1.0 / (pv[128:129] + shift[0])
        inv_b = 1.0 / (pv[128 + HALO:129 + HALO] + shift[1])
        on = pv[:128] * jnp.where(low_out, inv_a, inv_b)
        for c in range(B_Q_PER_KV):
            t = p * B_Q_PER_KV + c
            ocat_ref[qb * BLOCK:(qb + 1) * BLOCK, t * 128:(t + 1) * 128] = on[:, c * BLOCK:(c + 1) * BLOCK].T

    units = [(qb, p) for qb in range(bpt) for p in range(_PAIRS)]
    sc = scores(*units[0])
    for u, unit in enumerate(units):
        sc_next = scores(*units[u + 1]) if u + 1 < len(units) else None
        attend(*unit, sc)
        sc = sc_next
    m = jnp.dot(ocat_ref[...].astype(BF16), wo_ref[...], preferred_element_type=F32)
    o_ref[0] = x_ref[0] + mod_ref[0][2:3] * m


def _attn_layer(x, mod, q, k, vt, sinks, w_o):
    b, s, d = x.shape
    tile = min(TOKEN_TILE, s)
    bpt = tile // BLOCK
    nblk = s // BLOCK
    assert nblk >= 2
    tok = lambda bi, i: (bi, i, 0)
    prev = lambda bi, i: jnp.maximum(i * bpt - 1, 0)
    nxt = lambda bi, i: jnp.minimum((i + 1) * bpt, nblk - 1)
    bias = jnp.asarray(_attn_bias_table())
    head = np.array([[[(2 * p + half) * B_Q_PER_KV + c for c in range(B_Q_PER_KV)] for half in range(2)]
                     for p in range(_PAIRS)])
    sink_tbl = jnp.repeat(sinks.astype(F32)[head] * LOG2E, BLOCK, axis=-1)[:, :, None, :]
    return pl.pallas_call(
        functools.partial(_attn_kernel, tile=tile, nblk=nblk),
        grid=(b, s // tile),
        in_specs=[
            pl.BlockSpec((1, tile, d), tok),
            pl.BlockSpec((1, 6, d), lambda bi, i: (bi, 0, 0)),
            pl.BlockSpec((1, bpt, Q_WIDTH, 128), lambda bi, i: (bi, i, 0, 0)),
            pl.BlockSpec((1, BLOCK, KV_WIDTH), lambda bi, i: (bi, prev(bi, i), 0)),
            pl.BlockSpec((1, tile, KV_WIDTH), tok),
            pl.BlockSpec((1, BLOCK, KV_WIDTH), lambda bi, i: (bi, nxt(bi, i), 0)),
            pl.BlockSpec((1, KV_WIDTH, BLOCK), lambda bi, i: (bi, 0, prev(bi, i))),
            pl.BlockSpec((1, KV_WIDTH, tile), lambda bi, i: (bi, 0, i)),
            pl.BlockSpec((1, KV_WIDTH, BLOCK), lambda bi, i: (bi, 0, nxt(bi, i))),
            pl.BlockSpec(bias.shape, lambda bi, i: (0, 0, 0, 0)),
            pl.BlockSpec(sink_tbl.shape, lambda bi, i: (0, 0, 0, 0)),
            pl.BlockSpec((Q_WIDTH, d), lambda bi, i: (0, 0)),
        ],
        out_specs=pl.BlockSpec((1, tile, d), tok),
        out_shape=jax.ShapeDtypeStruct(x.shape, F32),
        scratch_shapes=[
            pltpu.VMEM((tile + 2 * BLOCK, KV_WIDTH), BF16),
            pltpu.VMEM((bpt + 2, KV_WIDTH, BLOCK), BF16),
            pltpu.VMEM((tile, Q_WIDTH), F32),
        ],
        compiler_params=_params("parallel", "parallel"),
        name="window_attention",
    )(x, mod, q, k, k, k, vt, vt, vt, bias, sink_tbl, w_o[_HEAD_COLS].astype(BF16))


def _dft_tables(seq):
    n1 = BLOCK
    n2 = seq // n1
    k1 = np.arange(n1, dtype=np.int64)
    s1 = np.arange(n1, dtype=np.int64)
    s2 = np.arange(n2, dtype=np.int64)
    pos = s1[None, None, :] * n2 + s2[:, None, None]
    ang = 2.0 * np.pi * ((k1[None, :, None] * pos) % seq).astype(np.float64) / seq
    stage1 = np.concatenate([np.cos(ang), -np.sin(ang)], axis=1).astype(np.float32)
    k2 = np.arange(n2, dtype=np.int64)
    ang2 = 2.0 * np.pi * ((k2[:, None] * s2[None, :]) % n2).astype(np.float64) / n2
    c2, sn2 = np.cos(ang2), np.sin(ang2)
    stage2 = np.block([[c2, sn2], [-sn2, c2]]).astype(np.float32)
    stage2 = stage2.reshape(2 * n2, 2, n2).transpose(0, 2, 1).reshape(2 * n2, 2 * n2)
    c = np.arange(C_GROUP_DIM, dtype=np.int64)
    angc = 2.0 * np.pi * ((c[:, None] * c[None, :]) % C_GROUP_DIM).astype(np.float64) / C_GROUP_DIM
    chan = (np.concatenate([np.cos(angc), np.sin(angc)], axis=0) / np.sqrt(float(seq) * C_GROUP_DIM)).astype(np.float32)
    return n1, n2, stage1, stage2, chan


def _fourier1_kernel(x_ref, mod_ref, g_ref, win_ref, dft_ref, mid_ref, *, n1, grp):
    mod = mod_ref[0]
    x = x_ref[0].reshape(n1 * grp, D_MODEL)
    h = _norm_mod(x, g_ref[...], mod[1:2], mod[0:1]).astype(BF16)
    z = jnp.dot(h, win_ref[...], preferred_element_type=F32)
    z = _regroup_rows(z, n1).astype(BF16)
    y = jnp.concatenate([jnp.dot(dft_ref[g], z[g * n1:(g + 1) * n1], preferred_element_type=F32)
                         for g in range(grp)], axis=0)
    mid_ref[0] = _regroup_rows(y, 2 * grp).astype(BF16).reshape(n1, 2 * grp, D_MODEL)


def _fourier2_kernel(x_ref, mod_ref, mid_ref, dft_ref, chan_ref, wout_ref, o_ref, re_ref, im_ref, f_ref, *, n2, grp):
    for kl in range(grp):
        spec = jnp.dot(dft_ref[...], mid_ref[0, kl], preferred_element_type=F32)
        re_ref[kl * n2:(kl + 1) * n2] = spec[:n2].astype(BF16)
        im_ref[kl * n2:(kl + 1) * n2] = spec[n2:].astype(BF16)
    for cg in range(C_GROUPS):
        cols = slice(cg * C_GROUP_DIM, (cg + 1) * C_GROUP_DIM)
        lhs = jnp.concatenate([re_ref[:, cols], im_ref[:, cols]], axis=1)
        f_ref[:, cols] = jnp.dot(lhs, chan_ref[...], preferred_element_type=F32).astype(BF16)
    m = jnp.dot(f_ref[...], wout_ref[...], preferred_element_type=F32)
    m = _regroup_rows(m, grp)
    out = x_ref[0].reshape(n2 * grp, D_MODEL) + mod_ref[0][2:3] * m
    o_ref[0] = out.reshape(n2, grp, D_MODEL)


def _fourier_layer(x, mod, gain, w_in, w_out):
    b, s, d = x.shape
    n1, n2, stage1, stage2, chan = _dft_tables(s)
    grp = HALO
    const2 = lambda bi, j: (0, 0)
    mid = pl.pallas_call(
        functools.partial(_fourier1_kernel, n1=n1, grp=grp),
        grid=(b, n2 // grp),
        in_specs=[
            pl.BlockSpec((1, n1, grp, d), lambda bi, j: (bi, 0, j, 0)),
            pl.BlockSpec((1, 6, d), lambda bi, j: (bi, 0, 0)),
            pl.BlockSpec((1, d), const2),
            pl.BlockSpec((d, d), const2),
            pl.BlockSpec((grp, 2 * n1, n1), lambda bi, j: (j, 0, 0)),
        ],
        out_specs=pl.BlockSpec((1, n1, 2 * grp, d), lambda bi, j: (bi, 0, j, 0)),
        out_shape=jax.ShapeDtypeStruct((b, n1, 2 * n2, d), BF16),
        compiler_params=_params("parallel", "parallel"),
        name="fourier_stage1",
    )(x.reshape(b, n1, n2, d), mod, gain.reshape(1, d), w_in.astype(BF16), jnp.asarray(stage1).astype(BF16))
    out = pl.pallas_call(
        functools.partial(_fourier2_kernel, n2=n2, grp=grp),
        grid=(b, n1 // grp),
        in_specs=[
            pl.BlockSpec((1, n2, grp, d), lambda bi, j: (bi, 0, j, 0)),
            pl.BlockSpec((1, 6, d), lambda bi, j: (bi, 0, 0)),
            pl.BlockSpec((1, grp, 2 * n2, d), lambda bi, j: (bi, j, 0, 0)),
            pl.BlockSpec((2 * n2, 2 * n2), const2),
            pl.BlockSpec((2 * C_GROUP_DIM, C_GROUP_DIM), const2),
            pl.BlockSpec((d, d), const2),
        ],
        out_specs=pl.BlockSpec((1, n2, grp, d), lambda bi, j: (bi, 0, j, 0)),
        out_shape=jax.ShapeDtypeStruct((b, n2, n1, d), F32),
        scratch_shapes=[pltpu.VMEM((grp * n2, d), BF16)] * 3,
        compiler_params=_params("parallel", "parallel"),
        name="fourier_stage2",
    )(x.reshape(b, n2, n1, d), mod, mid, jnp.asarray(stage2).astype(BF16),
      jnp.asarray(chan).astype(BF16), w_out.astype(BF16))
    return out.reshape(b, s, d)


def _ffn_kernel(xp_ref, x_ref, xn_ref, mod_ref, g_ref, wup_ref, wc_ref, bc_ref, wdn_ref, gf_ref, o_ref,
                h_ref, up_ref, act_ref, *, tile, final):
    i = pl.program_id(1)
    last = pl.num_programs(1) - 1
    mod = mod_ref[0]
    gain = g_ref[...]
    x = _regroup_rows(x_ref[0], HALO)
    h_ref[0:tile] = _norm_mod(x, gain, mod[4:5], mod[3:4]).astype(BF16)
    h_next = _norm_mod(xn_ref[0], gain, mod[4:5], mod[3:4]) * (i < last).astype(F32)
    h_prev = _norm_mod(xp_ref[0], gain, mod[4:5], mod[3:4]) * (i > 0).astype(F32)
    h_ref[tile:] = jnp.concatenate([h_next, h_prev], axis=0).astype(BF16)
    h = h_ref[...]
    n_chunks = FF_DIM // FF_CHUNK
    sub = lax.broadcasted_iota(jnp.int32, (HALO, FF_CHUNK), 0)

    def up_proj(j):
        for half in range(2):
            c0 = half * FF_DIM + j * FF_CHUNK
            up = jnp.dot(h, wup_ref[:, c0:c0 + FF_CHUNK], preferred_element_type=F32)
            dst = up_ref.at[j % 2, half]
            dst[HALO:HALO + tile] = up[:tile]
            dst[0:HALO] = jnp.where(sub == 0, pltpu.roll(up[tile + HALO:], 1, 0),
                                    pltpu.roll(up[tile - HALO:tile], 1, 0))
            dst[HALO + tile:] = jnp.where(sub == HALO - 1, pltpu.roll(up[tile:tile + HALO], HALO - 1, 0),
                                          pltpu.roll(up[0:HALO], HALO - 1, 0))

    def conv(j, half):
        cols = slice(half * FF_DIM + j * FF_CHUNK, half * FF_DIM + (j + 1) * FF_CHUNK)
        w = wc_ref[:, cols]
        src = up_ref.at[j % 2, half]
        return (src[0:tile] * w[0:1] + src[HALO:HALO + tile] * w[1:2] + src[2 * HALO:2 * HALO + tile] * w[2:3]
                + bc_ref[:, cols])

    up_proj(0)
    for j in range(n_chunks):
        if j + 1 < n_chunks:
            up_proj(j + 1)
        a = conv(j, 0)
        g = conv(j, 1)
        act_ref[:, j * FF_CHUNK:(j + 1) * FF_CHUNK] = (a * (g * jax.nn.sigmoid(g))).astype(BF16)
    y = x + mod[5:6] * jnp.dot(act_ref[...], wdn_ref[...], preferred_element_type=F32)
    if final:
        y = y * lax.rsqrt(jnp.mean(y * y, axis=-1, keepdims=True) + EPS) * gf_ref[...]
    o_ref[0] = _regroup_rows(y, tile // HALO)


def _ffn_layer(x, mod, gain, w_up, w_conv, b_conv, w_down, g_final, final):
    b, s, d = x.shape
    tile = min(2 * TOKEN_TILE, s)
    hpt = tile // HALO
    nh = s // HALO
    tok = lambda bi, i: (bi, i, 0)
    const2 = lambda bi, i: (0, 0)
    return pl.pallas_call(
        functools.partial(_ffn_kernel, tile=tile, final=final),
        grid=(b, s // tile),
        in_specs=[
            pl.BlockSpec((1, HALO, d), lambda bi, i: (bi, jnp.maximum(i * hpt - 1, 0), 0)),
            pl.BlockSpec((1, tile, d), tok),
            pl.BlockSpec((1, HALO, d), lambda bi, i: (bi, jnp.minimum((i + 1) * hpt, nh - 1), 0)),
            pl.BlockSpec((1, 6, d), lambda bi, i: (bi, 0, 0)),
            pl.BlockSpec((1, d), const2),
            pl.BlockSpec((d, 2 * FF_DIM), const2),
            pl.BlockSpec((3, 2 * FF_DIM), const2),
            pl.BlockSpec((1, 2 * FF_DIM), const2),
            pl.BlockSpec((FF_DIM, d), const2),
            pl.BlockSpec((1, d), const2),
        ],
        out_specs=pl.BlockSpec((1, tile, d), tok),
        out_shape=jax.ShapeDtypeStruct(x.shape, F32),
        scratch_shapes=[
            pltpu.VMEM((tile + 2 * HALO, d), BF16),
            pltpu.VMEM((2, 2, tile + 2 * HALO, FF_CHUNK), F32),
            pltpu.VMEM((tile, FF_DIM), BF16),
        ],
        compiler_params=_params("parallel", "parallel"),
        name="conv_ffn",
    )(x, x, x, mod, gain.reshape(1, d), w_up.astype(BF16), w_conv, b_conv.reshape(1, 2 * FF_DIM),
      w_down.astype(BF16), g_final.reshape(1, d))


def _trunk(x, mods, norm_g, a_w_in, a_g_v, a_w_s, a_b_s, a_w_out, b_w_qkv, b_sinks, b_w_o, c_w_in, c_w_out,
           f_w_up, f_w_conv, f_b_conv, f_w_down, g_final):
    for i in range(DEPTH):
        mod = mods[i]
        kind, j = i % N_MIXERS, i // N_MIXERS
        if kind == 0:
            x = _gmlp_layer(x, mod, norm_g[i, 0], a_w_in[j], a_g_v[j], a_w_s[j], a_b_s[j], a_w_out[j])
        elif kind == 1:
            q, k, v = _qkv_proj(x, mod, norm_g[i, 0], b_w_qkv[j])
            x = _attn_layer(x, mod, q, k, v, b_sinks[j], b_w_o[j])
        else:
            x = _fourier_layer(x, mod, norm_g[i, 0], c_w_in[j], c_w_out[j])
        x = _ffn_layer(x, mod, norm_g[i, 1], f_w_up[i], f_w_conv[i], f_b_conv[i], f_w_down[i], g_final,
                       final=(i == DEPTH - 1))
    return x


def kernel(x_prompt, x_sample, c_prompt, c_sample, w_ada, b_ada, norm_g, a_w_in, a_g_v, a_w_s, a_b_s, a_w_out,
           b_w_qkv, b_sinks, b_w_o, c_w_in, c_w_out, f_w_up, f_w_conv, f_b_conv, f_w_down, g_final):
    nb_p, nb_s = x_prompt.shape[0], x_sample.shape[0]
    assert nb_p + nb_s <= MOD_ROWS
    c_all = jnp.concatenate(
        [c_prompt, c_sample, jnp.zeros((MOD_ROWS - nb_p - nb_s, D_MODEL), F32)], axis=0)
    mods = _modulation(c_all, w_ada, b_ada).reshape(DEPTH, MOD_ROWS, 6, D_MODEL)
    weights = (norm_g, a_w_in, a_g_v, a_w_s, a_b_s, a_w_out, b_w_qkv, b_sinks, b_w_o, c_w_in, c_w_out,
               f_w_up, f_w_conv, f_b_conv, f_w_down, g_final)
    y_prompt = _trunk(x_prompt, mods[:, :nb_p], *weights)
    y_sample = _trunk(x_sample, mods[:, nb_p:nb_p + nb_s], *weights)
    return (y_prompt, y_sample)
```

```python
import functools

import numpy as np
import jax
import jax.numpy as jnp
from jax import lax
from jax.experimental import pallas as pl
from jax.experimental.pallas import tpu as pltpu

F32 = jnp.float32
BF16 = jnp.bfloat16

D_MODEL = 1024
DEPTH = 4
N_MIXERS = 3
EPS = 1e-6
NEG_INF = -1e30
CHUNK = 128
A_GROUPS = 8
A_HEAD = D_MODEL // A_GROUPS
B_HEADS = 16
B_KV_HEADS = 4
B_HEAD_DIM = 64
B_Q_PER_KV = B_HEADS // B_KV_HEADS
WINDOW = 128
BLOCK = 128
Q_WIDTH = B_HEADS * B_HEAD_DIM
KV_WIDTH = B_KV_HEADS * B_HEAD_DIM
C_GROUPS = 8
C_GROUP_DIM = D_MODEL // C_GROUPS
FF_DIM = 2816
FF_CHUNK = 256
HALO = 8

MOD_ROWS = 8
MOD_NT = 1536
VMEM_LIMIT = 56 * 1024 * 1024

TOKEN_TILE = 512


def _params(*sem):
    return pltpu.CompilerParams(dimension_semantics=sem, vmem_limit_bytes=VMEM_LIMIT)


def _norm_mod(x, gain, scale, shift):
    ms = jnp.mean(x * x, axis=-1, keepdims=True)
    return x * lax.rsqrt(ms + EPS) * (gain * (1.0 + scale)) + shift


def _regroup_rows(x, outer):
    rows, cols = x.shape
    return jnp.transpose(x.reshape(outer, rows // outer, cols), (1, 0, 2)).reshape(rows, cols)


def _mod_kernel(c_ref, w_ref, b_ref, o_ref):
    c = c_ref[...]
    cs = c * jax.nn.sigmoid(c)
    o_ref[0] = jnp.dot(cs, w_ref[0], preferred_element_type=F32) + b_ref[0]


def _modulation(c_all, w_ada, b_ada):
    n_out = w_ada.shape[-1]
    return pl.pallas_call(
        _mod_kernel,
        grid=(DEPTH, n_out // MOD_NT),
        in_specs=[
            pl.BlockSpec((MOD_ROWS, D_MODEL), lambda i, j: (0, 0)),
            pl.BlockSpec((1, D_MODEL, MOD_NT), lambda i, j: (i, 0, j)),
            pl.BlockSpec((1, 1, MOD_NT), lambda i, j: (i, 0, j)),
        ],
        out_specs=pl.BlockSpec((1, MOD_ROWS, MOD_NT), lambda i, j: (i, 0, j)),
        out_shape=jax.ShapeDtypeStruct((DEPTH, MOD_ROWS, n_out), F32),
        compiler_params=_params("parallel", "parallel"),
        name="modulation",
    )(c_all, w_ada, b_ada.reshape(DEPTH, 1, n_out))


def _gmlp_kernel(x_ref, mod_ref, g_ref, win_ref, gv_ref, ws_ref, bs_ref, wout_ref, o_ref, gated_ref, *, tile):
    mod = mod_ref[0]
    half = tile // 2
    n_sub = half // CHUNK
    halves = [slice(i * half, (i + 1) * half) for i in range(2)]
    hs = [_norm_mod(x_ref[0, r, :], g_ref[...], mod[1:2], mod[0:1]).astype(BF16) for r in halves]
    uvs = [jnp.dot(h, win_ref[...], preferred_element_type=F32) for h in hs]
    for i, r in enumerate(halves):
        uv = uvs[i]
        uv = 0.5 * uv * (1.0 + lax.erf(uv * np.float32(1.0 / np.sqrt(2.0))))
        u = uv[:, :D_MODEL]
        v = uv[:, D_MODEL:]
        v = v * lax.rsqrt(jnp.mean(v * v, axis=-1, keepdims=True) + EPS) * gv_ref[...]
        vb = v.astype(BF16)
        for g in range(A_GROUPS):
            cols = slice(g * A_HEAD, (g + 1) * A_HEAD)
            rhs = jnp.concatenate([vb[c * CHUNK:(c + 1) * CHUNK, cols] for c in range(n_sub)], axis=1)
            sv = jnp.dot(ws_ref[g], rhs, preferred_element_type=F32)
            for c in range(n_sub):
                rows = slice(c * CHUNK, (c + 1) * CHUNK)
                gated_ref[i, rows, cols] = (u[rows, cols] * (sv[:, c * CHUNK:(c + 1) * CHUNK] + bs_ref[g])).astype(BF16)
        m = jnp.dot(gated_ref[i], wout_ref[...], preferred_element_type=F32)
        o_ref[0, r, :] = x_ref[0, r, :] + mod[2:3] * m


def _gmlp_layer(x, mod, gain, w_in, g_v, w_s, b_s, w_out):
    b, s, d = x.shape
    tile = min(2 * TOKEN_TILE, s)
    bs_full = jnp.broadcast_to(b_s[:, :, None], (A_GROUPS, CHUNK, A_HEAD))
    const2 = lambda bi, i: (0, 0)
    const3 = lambda bi, i: (0, 0, 0)
    return pl.pallas_call(
        functools.partial(_gmlp_kernel, tile=tile),
        grid=(b, s // tile),
        in_specs=[
            pl.BlockSpec((1, tile, d), lambda bi, i: (bi, i, 0)),
            pl.BlockSpec((1, 6, d), lambda bi, i: (bi, 0, 0)),
            pl.BlockSpec((1, d), const2),
            pl.BlockSpec((d, 2 * d), const2),
            pl.BlockSpec((1, d), const2),
            pl.BlockSpec((A_GROUPS, CHUNK, CHUNK), const3),
            pl.BlockSpec((A_GROUPS, CHUNK, A_HEAD), const3),
            pl.BlockSpec((d, d), const2),
        ],
        out_specs=pl.BlockSpec((1, tile, d), lambda bi, i: (bi, i, 0)),
        out_shape=jax.ShapeDtypeStruct(x.shape, F32),
        scratch_shapes=[pltpu.VMEM((2, tile // 2, d), BF16)],
        compiler_params=_params("parallel", "parallel"),
        name="gmlp_mixer",
    )(x, mod, gain.reshape(1, d), w_in.astype(BF16), g_v.reshape(1, d), w_s.astype(BF16), bs_full,
      w_out.astype(BF16))


_PAIRS = B_KV_HEADS // 2
_HEAD_ORDER = [((2 * (t // B_Q_PER_KV) + half) * B_Q_PER_KV + t % B_Q_PER_KV)
               for t in range(B_HEADS // 2) for half in range(2)]
_HEAD_COLS = np.concatenate([np.arange(h * B_HEAD_DIM, (h + 1) * B_HEAD_DIM) for h in _HEAD_ORDER])
GROUP_ROWS = B_Q_PER_KV * BLOCK
BAND = 3 * BLOCK
LOG2E = float(np.log2(np.e))
_NT = (((1,), (1,)), ((), ()))


def _qkv_kernel(x_ref, mod_ref, g_ref, wqk_ref, wvt_ref, q_ref, k_ref, vt_ref, *, tile):
    mod = mod_ref[0]
    h = _norm_mod(x_ref[0], g_ref[...], mod[1:2], mod[0:1]).astype(BF16)
    qk = jnp.dot(h, wqk_ref[...], preferred_element_type=F32)
    for blk in range(tile // BLOCK):
        rows = slice(blk * BLOCK, (blk + 1) * BLOCK)
        for t in range(Q_WIDTH // 128):
            q_ref[0, blk, t * BLOCK:(t + 1) * BLOCK, :] = (
                qk[rows, t * 128:(t + 1) * 128] * (B_HEAD_DIM ** -0.5 * LOG2E)).astype(BF16)
    k_ref[0] = qk[:, Q_WIDTH:].astype(BF16)
    vt_ref[0] = lax.dot_general(wvt_ref[...], h, _NT, preferred_element_type=F32).astype(BF16)


def _qkv_proj(x, mod, gain, w_qkv):
    b, s, d = x.shape
    tile = min(TOKEN_TILE, s)
    tok = lambda bi, i: (bi, i, 0)
    const2 = lambda bi, i: (0, 0)
    w_qk = jnp.concatenate([w_qkv[:, _HEAD_COLS], w_qkv[:, Q_WIDTH:Q_WIDTH + KV_WIDTH]], axis=1).astype(BF16)
    w_vt = w_qkv[:, Q_WIDTH + KV_WIDTH:].T.astype(BF16)
    return pl.pallas_call(
        functools.partial(_qkv_kernel, tile=tile),
        grid=(b, s // tile),
        in_specs=[
            pl.BlockSpec((1, tile, d), tok),
            pl.BlockSpec((1, 6, d), lambda bi, i: (bi, 0, 0)),
            pl.BlockSpec((1, d), const2),
            pl.BlockSpec((d, Q_WIDTH + KV_WIDTH), const2),
            pl.BlockSpec((KV_WIDTH, d), const2),
        ],
        out_specs=[
            pl.BlockSpec((1, tile // BLOCK, Q_WIDTH, 128), lambda bi, i: (bi, i, 0, 0)),
            pl.BlockSpec((1, tile, KV_WIDTH), tok),
            pl.BlockSpec((1, KV_WIDTH, tile), lambda bi, i: (bi, 0, i)),
        ],
        out_shape=[
            jax.ShapeDtypeStruct((b, s // BLOCK, Q_WIDTH, 128), BF16),
            jax.ShapeDtypeStruct((b, s, KV_WIDTH), BF16),
            jax.ShapeDtypeStruct((b, KV_WIDTH, s), BF16),
        ],
        compiler_params=_params("parallel", "parallel"),
        name="qkv_proj",
    )(x, mod, gain.reshape(1, d), w_qk, w_vt)


def _attn_bias_table():
    qq = np.arange(BLOCK)[None, :]
    kk = np.arange(BAND)[:, None]
    dist = np.abs(qq + BLOCK - kk)
    slopes = np.exp2(np.float32(-8.0) * np.arange(1, B_HEADS + 1, dtype=np.float32) / np.float32(B_HEADS))
    tbl = np.empty((3, _PAIRS, 2 * BAND, GROUP_ROWS), np.float32)
    for case in range(3):
        valid = dist <= WINDOW
        if case == 0:
            valid = valid & (kk >= BLOCK)
        if case == 2:
            valid = valid & (kk < 2 * BLOCK)
        for p in range(_PAIRS):
            for c in range(B_Q_PER_KV):
                for half in range(2):
                    hd = (2 * p + half) * B_Q_PER_KV + c
                    alibi = (-slopes[hd].astype(np.float64) * dist * LOG2E).astype(np.float32)
                    tbl[case, p, half * BAND:(half + 1) * BAND, c * BLOCK:(c + 1) * BLOCK] = np.where(
                        valid, alibi, np.float32(NEG_INF))
    return tbl


def _attn_kernel(x_ref, mod_ref, q_ref, kp_ref, k_ref, kn_ref, vp_ref, v_ref, vn_ref, bias_ref, sink_ref, wo_ref,
                 o_ref, kext_ref, vext_ref, ocat_ref, *, tile, nblk):
    i = pl.program_id(1)
    bpt = tile // BLOCK
    kext_ref[0:BLOCK] = kp_ref[0]
    kext_ref[BLOCK:BLOCK + tile] = k_ref[0]
    kext_ref[BLOCK + tile:] = kn_ref[0]
    vext_ref[0] = vp_ref[0]
    for j in range(bpt):
        vext_ref[1 + j] = v_ref[0, :, j * BLOCK:(j + 1) * BLOCK]
    vext_ref[1 + bpt] = vn_ref[0]
    low_lane = lax.broadcasted_iota(jnp.int32, (BAND, 128), 1) < B_HEAD_DIM
    low_row = lax.broadcasted_iota(jnp.int32, (128, BAND), 0) < B_HEAD_DIM
    low_out = lax.broadcasted_iota(jnp.int32, (128, GROUP_ROWS), 0) < B_HEAD_DIM
    ones_row = lax.broadcasted_iota(jnp.int32, (2 * HALO, 2 * BAND), 0) < HALO
    first_half = lax.broadcasted_iota(jnp.int32, (2 * HALO, 2 * BAND), 1) < BAND
    ones = jnp.where(ones_row == first_half, 1.0, 0.0).astype(BF16)

    def scores(qb, p):
        rows = slice(p * 128, (p + 1) * 128)
        n = i * bpt + qb
        case = jnp.where(n == 0, 0, jnp.where(n == nblk - 1, 2, 1))
        kb = kext_ref[qb * BLOCK:qb * BLOCK + BAND, rows]
        zk = jnp.zeros_like(kb)
        kz = jnp.concatenate([jnp.where(low_lane, kb, zk), jnp.where(low_lane, zk, kb)], axis=0)
        q4 = q_ref[0, qb, p * GROUP_ROWS:(p + 1) * GROUP_ROWS, :]
        return lax.dot_general(kz, q4, _NT, preferred_element_type=F32) + bias_ref[case, p]

    def attend(qb, p, sc):
        rows = slice(p * 128, (p + 1) * 128)
        vt = jnp.concatenate([vext_ref[qb, rows, :], vext_ref[qb + 1, rows, :], vext_ref[qb + 2, rows, :]], axis=1)
        zv = jnp.zeros_like(vt)
        vz = jnp.concatenate([
            jnp.concatenate([jnp.where(low_row, vt, zv), jnp.where(low_row, zv, vt)], axis=1), ones], axis=0)
        probs = []
        shift = []
        for half in range(2):
            sh = sc[half * BAND:(half + 1) * BAND]
            sink = sink_ref[p, half]
            mx = jnp.maximum(jnp.max(sh, axis=0, keepdims=True), sink)
            probs.append(jnp.exp2(sh - mx).astype(BF16))
            shift.append(jnp.exp2(sink - mx))
        pv = jnp.dot(vz, jnp.concatenate(probs, axis=0), preferred_element_type=F32)
        inv_a = 1.0 / (pv[128:129] + shift[0])
        inv_b = 1.0 / (pv[128 + HALO:129 + HALO] + shift[1])
        on = pv[:128] * jnp.where(low_out, inv_a, inv_b)
        for c in range(B_Q_PER_KV):
            t = p * B_Q_PER_KV + c
            ocat_ref[qb * BLOCK:(qb + 1) * BLOCK, t * 128:(t + 1) * 128] = on[:, c * BLOCK:(c + 1) * BLOCK].T

    units = [(qb, p) for qb in range(bpt) for p in range(_PAIRS)]
    sc = scores(*units[0])
    for u, unit in enumerate(units):
        sc_next = scores(*units[u + 1]) if u + 1 < len(units) else None
        attend(*unit, sc)
        sc = sc_next
    m = jnp.dot(ocat_ref[...].astype(BF16), wo_ref[...], preferred_element_type=F32)
    o_ref[0] = x_ref[0] + mod_ref[0][2:3] * m


def _attn_layer(x, mod, q, k, vt, sinks, w_o):
    b, s, d = x.shape
    tile = min(TOKEN_TILE, s)
    bpt = tile // BLOCK
    nblk = s // BLOCK
    assert nblk >= 2
    tok = lambda bi, i: (bi, i, 0)
    prev = lambda bi, i: jnp.maximum(i * bpt - 1, 0)
    nxt = lambda bi, i: jnp.minimum((i + 1) * bpt, nblk - 1)
    bias = jnp.asarray(_attn_bias_table())
    head = np.array([[[(2 * p + half) * B_Q_PER_KV + c for c in range(B_Q_PER_KV)] for half in range(2)]
                     for p in range(_PAIRS)])
    sink_tbl = jnp.repeat(sinks.astype(F32)[head] * LOG2E, BLOCK, axis=-1)[:, :, None, :]
    return pl.pallas_call(
        functools.partial(_attn_kernel, tile=tile, nblk=nblk),
        grid=(b, s // tile),
        in_specs=[
            pl.BlockSpec((1, tile, d), tok),
            pl.BlockSpec((1, 6, d), lambda bi, i: (bi, 0, 0)),
            pl.BlockSpec((1, bpt, Q_WIDTH, 128), lambda bi, i: (bi, i, 0, 0)),
            pl.BlockSpec((1, BLOCK, KV_WIDTH), lambda bi, i: (bi, prev(bi, i), 0)),
            pl.BlockSpec((1, tile, KV_WIDTH), tok),
            pl.BlockSpec((1, BLOCK, KV_WIDTH), lambda bi, i: (bi, nxt(bi, i), 0)),
            pl.BlockSpec((1, KV_WIDTH, BLOCK), lambda bi, i: (bi, 0, prev(bi, i))),
            pl.BlockSpec((1, KV_WIDTH, tile), lambda bi, i: (bi, 0, i)),
            pl.BlockSpec((1, KV_WIDTH, BLOCK), lambda bi, i: (bi, 0, nxt(bi, i))),
            pl.BlockSpec(bias.shape, lambda bi, i: (0, 0, 0, 0)),
            pl.BlockSpec(sink_tbl.shape, lambda bi, i: (0, 0, 0, 0)),
            pl.BlockSpec((Q_WIDTH, d), lambda bi, i: (0, 0)),
        ],
        out_specs=pl.BlockSpec((1, tile, d), tok),
        out_shape=jax.ShapeDtypeStruct(x.shape, F32),
        scratch_shapes=[
            pltpu.VMEM((tile + 2 * BLOCK, KV_WIDTH), BF16),
            pltpu.VMEM((bpt + 2, KV_WIDTH, BLOCK), BF16),
            pltpu.VMEM((tile, Q_WIDTH), F32),
        ],
        compiler_params=_params("parallel", "parallel"),
        name="window_attention",
    )(x, mod, q, k, k, k, vt, vt, vt, bias, sink_tbl, w_o[_HEAD_COLS].astype(BF16))


def _dft_tables(seq):
    n1 = BLOCK
    n2 = seq // n1
    k1 = np.arange(n1, dtype=np.int64)
    s1 = np.arange(n1, dtype=np.int64)
    s2 = np.arange(n2, dtype=np.int64)
    pos = s1[None, None, :] * n2 + s2[:, None, None]
    ang = 2.0 * np.pi * ((k1[None, :, None] * pos) % seq).astype(np.float64) / seq
    stage1 = np.concatenate([np.cos(ang), -np.sin(ang)], axis=1).astype(np.float32)
    k2 = np.arange(n2, dtype=np.int64)
    ang2 = 2.0 * np.pi * ((k2[:, None] * s2[None, :]) % n2).astype(np.float64) / n2
    c2, sn2 = np.cos(ang2), np.sin(ang2)
    stage2 = np.block([[c2, sn2], [-sn2, c2]]).astype(np.float32)
    stage2 = stage2.reshape(2 * n2, 2, n2).transpose(0, 2, 1).reshape(2 * n2, 2 * n2)
    c = np.arange(C_GROUP_DIM, dtype=np.int64)
    angc = 2.0 * np.pi * ((c[:, None] * c[None, :]) % C_GROUP_DIM).astype(np.float64) / C_GROUP_DIM
    chan = (np.concatenate([np.cos(angc), np.sin(angc)], axis=0) / np.sqrt(float(seq) * C_GROUP_DIM)).astype(np.float32)
    return n1, n2, stage1, stage2, chan


def _fourier1_kernel(x_ref, mod_ref, g_ref, win_ref, dft_ref, mid_ref, *, n1, grp):
    mod = mod_ref[0]
    x = x_ref[0].reshape(n1 * grp, D_MODEL)
    h = _norm_mod(x, g_ref[...], mod[1:2], mod[0:1]).astype(BF16)
    z = jnp.dot(h, win_ref[...], preferred_element_type=F32)
    z = _regroup_rows(z, n1).astype(BF16)
    y = jnp.concatenate([jnp.dot(dft_ref[g], z[g * n1:(g + 1) * n1], preferred_element_type=F32)
                         for g in range(grp)], axis=0)
    mid_ref[0] = _regroup_rows(y, 2 * grp).astype(BF16).reshape(n1, 2 * grp, D_MODEL)


def _fourier2_kernel(x_ref, mod_ref, mid_ref, dft_ref, chan_ref, wout_ref, o_ref, re_ref, im_ref, f_ref, *, n2, grp):
    for kl in range(grp):
        spec = jnp.dot(dft_ref[...], mid_ref[0, kl], preferred_element_type=F32)
        re_ref[kl * n2:(kl + 1) * n2] = spec[:n2].astype(BF16)
        im_ref[kl * n2:(kl + 1) * n2] = spec[n2:].astype(BF16)
    for cg in range(C_GROUPS):
        cols = slice(cg * C_GROUP_DIM, (cg + 1) * C_GROUP_DIM)
        lhs = jnp.concatenate([re_ref[:, cols], im_ref[:, cols]], axis=1)
        f_ref[:, cols] = jnp.dot(lhs, chan_ref[...], preferred_element_type=F32).astype(BF16)
    m = jnp.dot(f_ref[...], wout_ref[...], preferred_element_type=F32)
    m = _regroup_rows(m, grp)
    out = x_ref[0].reshape(n2 * grp, D_MODEL) + mod_ref[0][2:3] * m
    o_ref[0] = out.reshape(n2, grp, D_MODEL)


def _fourier_layer(x, mod, gain, w_in, w_out):
    b, s, d = x.shape
    n1, n2, stage1, stage2, chan = _dft_tables(s)
    grp = HALO
    const2 = lambda bi, j: (0, 0)
    mid = pl.pallas_call(
        functools.partial(_fourier1_kernel, n1=n1, grp=grp),
        grid=(b, n2 // grp),
        in_specs=[
            pl.BlockSpec((1, n1, grp, d), lambda bi, j: (bi, 0, j, 0)),
            pl.BlockSpec((1, 6, d), lambda bi, j: (bi, 0, 0)),
            pl.BlockSpec((1, d), const2),
            pl.BlockSpec((d, d), const2),
            pl.BlockSpec((grp, 2 * n1, n1), lambda bi, j: (j, 0, 0)),
        ],
        out_specs=pl.BlockSpec((1, n1, 2 * grp, d), lambda bi, j: (bi, 0, j, 0)),
        out_shape=jax.ShapeDtypeStruct((b, n1, 2 * n2, d), BF16),
        compiler_params=_params("parallel", "parallel"),
        name="fourier_stage1",
    )(x.reshape(b, n1, n2, d), mod, gain.reshape(1, d), w_in.astype(BF16), jnp.asarray(stage1).astype(BF16))
    out = pl.pallas_call(
        functools.partial(_fourier2_kernel, n2=n2, grp=grp),
        grid=(b, n1 // grp),
        in_specs=[
            pl.BlockSpec((1, n2, grp, d), lambda bi, j: (bi, 0, j, 0)),
            pl.BlockSpec((1, 6, d), lambda bi, j: (bi, 0, 0)),
            pl.BlockSpec((1, grp, 2 * n2, d), lambda bi, j: (bi, j, 0, 0)),
            pl.BlockSpec((2 * n2, 2 * n2), const2),
            pl.BlockSpec((2 * C_GROUP_DIM, C_GROUP_DIM), const2),
            pl.BlockSpec((d, d), const2),
        ],
        out_specs=pl.BlockSpec((1, n2, grp, d), lambda bi, j: (bi, 0, j, 0)),
        out_shape=jax.ShapeDtypeStruct((b, n2, n1, d), F32),
        scratch_shapes=[pltpu.VMEM((grp * n2, d), BF16)] * 3,
        compiler_params=_params("parallel", "parallel"),
        name="fourier_stage2",
    )(x.reshape(b, n2, n1, d), mod, mid, jnp.asarray(stage2).astype(BF16),
      jnp.asarray(chan).astype(BF16), w_out.astype(BF16))
    return out.reshape(b, s, d)


def _ffn_kernel(xp_ref, x_ref, xn_ref, mod_ref, g_ref, wup_ref, wc_ref, bc_ref, wdn_ref, gf_ref, o_ref,
                h_ref, up_ref, act_ref, *, tile, final):
    i = pl.program_id(1)
    last = pl.num_programs(1) - 1
    mod = mod_ref[0]
    gain = g_ref[...]
    x = _regroup_rows(x_ref[0], HALO)
    h_ref[0:tile] = _norm_mod(x, gain, mod[4:5], mod[3:4]).astype(BF16)
    h_next = _norm_mod(xn_ref[0], gain, mod[4:5], mod[3:4]) * (i < last).astype(F32)
    h_prev = _norm_mod(xp_ref[0], gain, mod[4:5], mod[3:4]) * (i > 0).astype(F32)
    h_ref[tile:] = jnp.concatenate([h_next, h_prev], axis=0).astype(BF16)
    h = h_ref[...]
    n_chunks = FF_DIM // FF_CHUNK
    sub = lax.broadcasted_iota(jnp.int32, (HALO, FF_CHUNK), 0)

    def up_proj(j):
        for half in range(2):
            c0 = half * FF_DIM + j * FF_CHUNK
            up = jnp.dot(h, wup_ref[:, c0:c0 + FF_CHUNK], preferred_element_type=F32)
            dst = up_ref.at[j % 2, half]
            dst[HALO:HALO + tile] = up[:tile]
            dst[0:HALO] = jnp.where(sub == 0, pltpu.roll(up[tile + HALO:], 1, 0),
                                    pltpu.roll(up[tile - HALO:tile], 1, 0))
            dst[HALO + tile:] = jnp.where(sub == HALO - 1, pltpu.roll(up[tile:tile + HALO], HALO - 1, 0),
                                          pltpu.roll(up[0:HALO], HALO - 1, 0))

    def conv(j, half):
        cols = slice(half * FF_DIM + j * FF_CHUNK, half * FF_DIM + (j + 1) * FF_CHUNK)
        w = wc_ref[:, cols]
        src = up_ref.at[j % 2, half]
        return (src[0:tile] * w[0:1] + src[HALO:HALO + tile] * w[1:2] + src[2 * HALO:2 * HALO + tile] * w[2:3]
                + bc_ref[:, cols])

    up_proj(0)
    for j in range(n_chunks):
        if j + 1 < n_chunks:
            up_proj(j + 1)
        a = conv(j, 0)
        g = conv(j, 1)
        act_ref[:, j * FF_CHUNK:(j + 1) * FF_CHUNK] = (a * (g * jax.nn.sigmoid(g))).astype(BF16)
    y = x + mod[5:6] * jnp.dot(act_ref[...], wdn_ref[...], preferred_element_type=F32)
    if final:
        y = y * lax.rsqrt(jnp.mean(y * y, axis=-1, keepdims=True) + EPS) * gf_ref[...]
    o_ref[0] = _regroup_rows(y, tile // HALO)


def _ffn_layer(x, mod, gain, w_up_all, w_conv, b_conv, w_down_all, g_final, layer, final):
    b, s, d = x.shape
    tile = min(2 * TOKEN_TILE, s)
    hpt = tile // HALO
    nh = s // HALO
    tok = lambda bi, i: (bi, i, 0)
    const2 = lambda bi, i: (0, 0)
    return pl.pallas_call(
        functools.partial(_ffn_kernel, tile=tile, final=final),
        grid=(b, s // tile),
        in_specs=[
            pl.BlockSpec((1, HALO, d), lambda bi, i: (bi, jnp.maximum(i * hpt - 1, 0), 0)),
            pl.BlockSpec((1, tile, d), tok),
            pl.BlockSpec((1, HALO, d), lambda bi, i: (bi, jnp.minimum((i + 1) * hpt, nh - 1), 0)),
            pl.BlockSpec((1, 6, d), lambda bi, i: (bi, 0, 0)),
            pl.BlockSpec((1, d), const2),
            pl.BlockSpec((None, d, 2 * FF_DIM), lambda bi, i: (layer, 0, 0)),
            pl.BlockSpec((3, 2 * FF_DIM), const2),
            pl.BlockSpec((1, 2 * FF_DIM), const2),
            pl.BlockSpec((None, FF_DIM, d), lambda bi, i: (layer, 0, 0)),
            pl.BlockSpec((1, d), const2),
        ],
        out_specs=pl.BlockSpec((1, tile, d), tok),
        out_shape=jax.ShapeDtypeStruct(x.shape, F32),
        scratch_shapes=[
            pltpu.VMEM((tile + 2 * HALO, d), BF16),
            pltpu.VMEM((2, 2, tile + 2 * HALO, FF_CHUNK), F32),
            pltpu.VMEM((tile, FF_DIM), BF16),
        ],
        compiler_params=_params("parallel", "parallel"),
        name="conv_ffn",
    )(x, x, x, mod, gain.reshape(1, d), w_up_all, w_conv, b_conv.reshape(1, 2 * FF_DIM), w_down_all,
      g_final.reshape(1, d))


def _trunk(x, mods, norm_g, a_w_in, a_g_v, a_w_s, a_b_s, a_w_out, b_w_qkv, b_sinks, b_w_o, c_w_in, c_w_out,
           f_w_up, f_w_conv, f_b_conv, f_w_down, g_final):
    for i in range(DEPTH):
        mod = mods[i]
        kind, j = i % N_MIXERS, i // N_MIXERS
        if kind == 0:
            x = _gmlp_layer(x, mod, norm_g[i, 0], a_w_in[j], a_g_v[j], a_w_s[j], a_b_s[j], a_w_out[j])
        elif kind == 1:
            q, k, v = _qkv_proj(x, mod, norm_g[i, 0], b_w_qkv[j])
            x = _attn_layer(x, mod, q, k, v, b_sinks[j], b_w_o[j])
        else:
            x = _fourier_layer(x, mod, norm_g[i, 0], c_w_in[j], c_w_out[j])
        x = _ffn_layer(x, mod, norm_g[i, 1], f_w_up, f_w_conv[i], f_b_conv[i], f_w_down, g_final, layer=i,
                       final=(i == DEPTH - 1))
    return x


def kernel(x_prompt, x_sample, c_prompt, c_sample, w_ada, b_ada, norm_g, a_w_in, a_g_v, a_w_s, a_b_s, a_w_out,
           b_w_qkv, b_sinks, b_w_o, c_w_in, c_w_out, f_w_up, f_w_conv, f_b_conv, f_w_down, g_final):
    nb_p, nb_s = x_prompt.shape[0], x_sample.shape[0]
    assert nb_p + nb_s <= MOD_ROWS
    c_all = jnp.concatenate(
        [c_prompt, c_sample, jnp.zeros((MOD_ROWS - nb_p - nb_s, D_MODEL), F32)], axis=0)
    mods = _modulation(c_all, w_ada, b_ada).reshape(DEPTH, MOD_ROWS, 6, D_MODEL)
    weights = (norm_g, a_w_in, a_g_v, a_w_s, a_b_s, a_w_out, b_w_qkv, b_sinks, b_w_o, c_w_in, c_w_out,
               f_w_up.astype(BF16), f_w_conv, f_b_conv, f_w_down.astype(BF16), g_final)
    y_prompt = _trunk(x_prompt, mods[:, :nb_p], *weights)
    y_sample = _trunk(x_sample, mods[:, nb_p:nb_p + nb_s], *weights)
    return (y_prompt, y_sample)
```

```python
import functools

import numpy as np
import jax
import jax.numpy as jnp
from jax import lax
from jax.experimental import pallas as pl
from jax.experimental.pallas import tpu as pltpu

F32 = jnp.float32
BF16 = jnp.bfloat16

D_MODEL = 1024
DEPTH = 4
N_MIXERS = 3
EPS = 1e-6
NEG_INF = -1e30
CHUNK = 128
A_GROUPS = 8
A_HEAD = D_MODEL // A_GROUPS
B_HEADS = 16
B_KV_HEADS = 4
B_HEAD_DIM = 64
B_Q_PER_KV = B_HEADS // B_KV_HEADS
WINDOW = 128
BLOCK = 128
Q_WIDTH = B_HEADS * B_HEAD_DIM
KV_WIDTH = B_KV_HEADS * B_HEAD_DIM
C_GROUPS = 8
C_GROUP_DIM = D_MODEL // C_GROUPS
FF_DIM = 2816
FF_CHUNK = 256
HALO = 8

MOD_ROWS = 8
MOD_NT = 1536
VMEM_LIMIT = 56 * 1024 * 1024

TOKEN_TILE = 512


def _params(*sem):
    return pltpu.CompilerParams(dimension_semantics=sem, vmem_limit_bytes=VMEM_LIMIT)


def _norm_mod(x, gain, scale, shift):
    ms = jnp.mean(x * x, axis=-1, keepdims=True)
    return x * lax.rsqrt(ms + EPS) * (gain * (1.0 + scale)) + shift


def _regroup_rows(x, outer):
    rows, cols = x.shape
    return jnp.transpose(x.reshape(outer, rows // outer, cols), (1, 0, 2)).reshape(rows, cols)


def _mod_kernel(c_ref, w_ref, b_ref, o_ref):
    c = c_ref[...]
    cs = c * jax.nn.sigmoid(c)
    o_ref[0] = jnp.dot(cs, w_ref[0], preferred_element_type=F32) + b_ref[0]


def _modulation(c_all, w_ada, b_ada):
    n_out = w_ada.shape[-1]
    return pl.pallas_call(
        _mod_kernel,
        grid=(DEPTH, n_out // MOD_NT),
        in_specs=[
            pl.BlockSpec((MOD_ROWS, D_MODEL), lambda i, j: (0, 0)),
            pl.BlockSpec((1, D_MODEL, MOD_NT), lambda i, j: (i, 0, j)),
            pl.BlockSpec((1, 1, MOD_NT), lambda i, j: (i, 0, j)),
        ],
        out_specs=pl.BlockSpec((1, MOD_ROWS, MOD_NT), lambda i, j: (i, 0, j)),
        out_shape=jax.ShapeDtypeStruct((DEPTH, MOD_ROWS, n_out), F32),
        compiler_params=_params("parallel", "parallel"),
        name="modulation",
    )(c_all, w_ada, b_ada.reshape(DEPTH, 1, n_out))


def _gmlp_kernel(x_ref, mod_ref, g_ref, win_ref, gv_ref, ws_ref, bs_ref, wout_ref, o_ref, gated_ref, *, tile):
    mod = mod_ref[0]
    half = tile // 2
    n_sub = half // CHUNK
    halves = [slice(i * half, (i + 1) * half) for i in range(2)]
    hs = [_norm_mod(x_ref[0, r, :], g_ref[...], mod[1:2], mod[0:1]).astype(BF16) for r in halves]
    uvs = [jnp.dot(h, win_ref[...], preferred_element_type=F32) for h in hs]
    for i, r in enumerate(halves):
        uv = uvs[i]
        uv = 0.5 * uv * (1.0 + lax.erf(uv * np.float32(1.0 / np.sqrt(2.0))))
        u = uv[:, :D_MODEL]
        v = uv[:, D_MODEL:]
        v = v * lax.rsqrt(jnp.mean(v * v, axis=-1, keepdims=True) + EPS) * gv_ref[...]
        vb = v.astype(BF16)
        for g in range(A_GROUPS):
            cols = slice(g * A_HEAD, (g + 1) * A_HEAD)
            rhs = jnp.concatenate([vb[c * CHUNK:(c + 1) * CHUNK, cols] for c in range(n_sub)], axis=1)
            sv = jnp.dot(ws_ref[g], rhs, preferred_element_type=F32)
            for c in range(n_sub):
                rows = slice(c * CHUNK, (c + 1) * CHUNK)
                gated_ref[i, rows, cols] = (u[rows, cols] * (sv[:, c * CHUNK:(c + 1) * CHUNK] + bs_ref[g])).astype(BF16)
        m = jnp.dot(gated_ref[i], wout_ref[...], preferred_element_type=F32)
        o_ref[0, r, :] = x_ref[0, r, :] + mod[2:3] * m


def _gmlp_layer(x, mod, gain, w_in, g_v, w_s, b_s, w_out):
    b, s, d = x.shape
    tile = min(2 * TOKEN_TILE, s)
    bs_full = jnp.broadcast_to(b_s[:, :, None], (A_GROUPS, CHUNK, A_HEAD))
    const2 = lambda bi, i: (0, 0)
    const3 = lambda bi, i: (0, 0, 0)
    return pl.pallas_call(
        functools.partial(_gmlp_kernel, tile=tile),
        grid=(b, s // tile),
        in_specs=[
            pl.BlockSpec((1, tile, d), lambda bi, i: (bi, i, 0)),
            pl.BlockSpec((1, 6, d), lambda bi, i: (bi, 0, 0)),
            pl.BlockSpec((1, d), const2),
            pl.BlockSpec((d, 2 * d), const2),
            pl.BlockSpec((1, d), const2),
            pl.BlockSpec((A_GROUPS, CHUNK, CHUNK), const3),
            pl.BlockSpec((A_GROUPS, CHUNK, A_HEAD), const3),
            pl.BlockSpec((d, d), const2),
        ],
        out_specs=pl.BlockSpec((1, tile, d), lambda bi, i: (bi, i, 0)),
        out_shape=jax.ShapeDtypeStruct(x.shape, F32),
        scratch_shapes=[pltpu.VMEM((2, tile // 2, d), BF16)],
        compiler_params=_params("parallel", "parallel"),
        name="gmlp_mixer",
    )(x, mod, gain.reshape(1, d), w_in.astype(BF16), g_v.reshape(1, d), w_s.astype(BF16), bs_full,
      w_out.astype(BF16))


_PAIRS = B_KV_HEADS // 2
_HEAD_ORDER = [((2 * (t // B_Q_PER_KV) + half) * B_Q_PER_KV + t % B_Q_PER_KV)
               for t in range(B_HEADS // 2) for half in range(2)]
_HEAD_COLS = np.concatenate([np.arange(h * B_HEAD_DIM, (h + 1) * B_HEAD_DIM) for h in _HEAD_ORDER])
GROUP_ROWS = B_Q_PER_KV * BLOCK
BAND = 3 * BLOCK
LOG2E = float(np.log2(np.e))
_NT = (((1,), (1,)), ((), ()))


def _qkv_kernel(x_ref, mod_ref, g_ref, wqk_ref, wvt_ref, q_ref, k_ref, vt_ref, *, tile):
    mod = mod_ref[0]
    h = _norm_mod(x_ref[0], g_ref[...], mod[1:2], mod[0:1]).astype(BF16)
    qk = jnp.dot(h, wqk_ref[...], preferred_element_type=F32)
    for blk in range(tile // BLOCK):
        rows = slice(blk * BLOCK, (blk + 1) * BLOCK)
        for t in range(Q_WIDTH // 128):
            q_ref[0, blk, t * BLOCK:(t + 1) * BLOCK, :] = (
                qk[rows, t * 128:(t + 1) * 128] * (B_HEAD_DIM ** -0.5 * LOG2E)).astype(BF16)
    k_ref[0] = qk[:, Q_WIDTH:].astype(BF16)
    vt_ref[0] = lax.dot_general(wvt_ref[...], h, _NT, preferred_element_type=F32).astype(BF16)


def _qkv_proj(x, mod, gain, w_qkv):
    b, s, d = x.shape
    tile = min(TOKEN_TILE, s)
    tok = lambda bi, i: (bi, i, 0)
    const2 = lambda bi, i: (0, 0)
    w_qk = jnp.concatenate([w_qkv[:, _HEAD_COLS], w_qkv[:, Q_WIDTH:Q_WIDTH + KV_WIDTH]], axis=1).astype(BF16)
    w_vt = w_qkv[:, Q_WIDTH + KV_WIDTH:].T.astype(BF16)
    return pl.pallas_call(
        functools.partial(_qkv_kernel, tile=tile),
        grid=(b, s // tile),
        in_specs=[
            pl.BlockSpec((1, tile, d), tok),
            pl.BlockSpec((1, 6, d), lambda bi, i: (bi, 0, 0)),
            pl.BlockSpec((1, d), const2),
            pl.BlockSpec((d, Q_WIDTH + KV_WIDTH), const2),
            pl.BlockSpec((KV_WIDTH, d), const2),
        ],
        out_specs=[
            pl.BlockSpec((1, tile // BLOCK, Q_WIDTH, 128), lambda bi, i: (bi, i, 0, 0)),
            pl.BlockSpec((1, tile, KV_WIDTH), tok),
            pl.BlockSpec((1, KV_WIDTH, tile), lambda bi, i: (bi, 0, i)),
        ],
        out_shape=[
            jax.ShapeDtypeStruct((b, s // BLOCK, Q_WIDTH, 128), BF16),
            jax.ShapeDtypeStruct((b, s, KV_WIDTH), BF16),
            jax.ShapeDtypeStruct((b, KV_WIDTH, s), BF16),
        ],
        compiler_params=_params("parallel", "parallel"),
        name="qkv_proj",
    )(x, mod, gain.reshape(1, d), w_qk, w_vt)


def _attn_bias_table():
    qq = np.arange(BLOCK)[None, :]
    kk = np.arange(BAND)[:, None]
    dist = np.abs(qq + BLOCK - kk)
    slopes = np.exp2(np.float32(-8.0) * np.arange(1, B_HEADS + 1, dtype=np.float32) / np.float32(B_HEADS))
    tbl = np.empty((3, _PAIRS, 2 * BAND, GROUP_ROWS), np.float32)
    for case in range(3):
        valid = dist <= WINDOW
        if case == 0:
            valid = valid & (kk >= BLOCK)
        if case == 2:
            valid = valid & (kk < 2 * BLOCK)
        for p in range(_PAIRS):
            for c in range(B_Q_PER_KV):
                for half in range(2):
                    hd = (2 * p + half) * B_Q_PER_KV + c
                    alibi = (-slopes[hd].astype(np.float64) * dist * LOG2E).astype(np.float32)
                    tbl[case, p, half * BAND:(half + 1) * BAND, c * BLOCK:(c + 1) * BLOCK] = np.where(
                        valid, alibi, np.float32(NEG_INF))
    return tbl


def _attn_kernel(x_ref, mod_ref, q_ref, kp_ref, k_ref, kn_ref, vp_ref, v_ref, vn_ref, bias_ref, sink_ref, wo_ref,
                 o_ref, kext_ref, vext_ref, ocat_ref, *, tile, nblk):
    i = pl.program_id(1)
    bpt = tile // BLOCK
    kext_ref[0:BLOCK] = kp_ref[0]
    kext_ref[BLOCK:BLOCK + tile] = k_ref[0]
    kext_ref[BLOCK + tile:] = kn_ref[0]
    vext_ref[0] = vp_ref[0]
    for j in range(bpt):
        vext_ref[1 + j] = v_ref[0, :, j * BLOCK:(j + 1) * BLOCK]
    vext_ref[1 + bpt] = vn_ref[0]
    low_lane = lax.broadcasted_iota(jnp.int32, (BAND, 128), 1) < B_HEAD_DIM
    low_row = lax.broadcasted_iota(jnp.int32, (128, BAND), 0) < B_HEAD_DIM
    low_out = lax.broadcasted_iota(jnp.int32, (128, GROUP_ROWS), 0) < B_HEAD_DIM
    ones_row = lax.broadcasted_iota(jnp.int32, (2 * HALO, 2 * BAND), 0) < HALO
    first_half = lax.broadcasted_iota(jnp.int32, (2 * HALO, 2 * BAND), 1) < BAND
    ones = jnp.where(ones_row == first_half, 1.0, 0.0).astype(BF16)

    def scores(qb, p):
        rows = slice(p * 128, (p + 1) * 128)
        n = i * bpt + qb
        case = jnp.where(n == 0, 0, jnp.where(n == nblk - 1, 2, 1))
        kb = kext_ref[qb * BLOCK:qb * BLOCK + BAND, rows]
        zk = jnp.zeros_like(kb)
        kz = jnp.concatenate([jnp.where(low_lane, kb, zk), jnp.where(low_lane, zk, kb)], axis=0)
        q4 = q_ref[0, qb, p * GROUP_ROWS:(p + 1) * GROUP_ROWS, :]
        return lax.dot_general(kz, q4, _NT, preferred_element_type=F32) + bias_ref[case, p]

    def attend(qb, p, sc):
        rows = slice(p * 128, (p + 1) * 128)
        vt = jnp.concatenate([vext_ref[qb, rows, :], vext_ref[qb + 1, rows, :], vext_ref[qb + 2, rows, :]], axis=1)
        zv = jnp.zeros_like(vt)
        vz = jnp.concatenate([
            jnp.concatenate([jnp.where(low_row, vt, zv), jnp.where(low_row, zv, vt)], axis=1), ones], axis=0)
        probs = []
        shift = []
        for half in range(2):
            sh = sc[half * BAND:(half + 1) * BAND]
            sink = sink_ref[p, half]
            mx = jnp.maximum(jnp.max(sh, axis=0, keepdims=True), sink)
            probs.append(jnp.exp2(sh - mx).astype(BF16))
            shift.append(jnp.exp2(sink - mx))
        pv = jnp.dot(vz, jnp.concatenate(probs, axis=0), preferred_element_type=F32)
        inv_a = 1.0 / (pv[128:129] + shift[0])
        inv_b = 1.0 / (pv[128 + HALO:129 + HALO] + shift[1])
        on = pv[:128] * jnp.where(low_out, inv_a, inv_b)
        for c in range(B_Q_PER_KV):
            t = p * B_Q_PER_KV + c
            ocat_ref[qb * BLOCK:(qb + 1) * BLOCK, t * 128:(t + 1) * 128] = on[:, c * BLOCK:(c + 1) * BLOCK].T

    units = [(qb, p) for qb in range(bpt) for p in range(_PAIRS)]
    sc = scores(*units[0])
    for u, unit in enumerate(units):
        sc_next = scores(*units[u + 1]) if u + 1 < len(units) else None
        attend(*unit, sc)
        sc = sc_next
    m = jnp.dot(ocat_ref[...].astype(BF16), wo_ref[...], preferred_element_type=F32)
    o_ref[0] = x_ref[0] + mod_ref[0][2:3] * m


def _attn_layer(x, mod, q, k, vt, sinks, w_o):
    b, s, d = x.shape
    tile = min(TOKEN_TILE, s)
    bpt = tile // BLOCK
    nblk = s // BLOCK
    assert nblk >= 2
    tok = lambda bi, i: (bi, i, 0)
    prev = lambda bi, i: jnp.maximum(i * bpt - 1, 0)
    nxt = lambda bi, i: jnp.minimum((i + 1) * bpt, nblk - 1)
    bias = jnp.asarray(_attn_bias_table())
    head = np.array([[[(2 * p + half) * B_Q_PER_KV + c for c in range(B_Q_PER_KV)] for half in range(2)]
                     for p in range(_PAIRS)])
    sink_tbl = jnp.repeat(sinks.astype(F32)[head] * LOG2E, BLOCK, axis=-1)[:, :, None, :]
    return pl.pallas_call(
        functools.partial(_attn_kernel, tile=tile, nblk=nblk),
        grid=(b, s // tile),
        in_specs=[
            pl.BlockSpec((1, tile, d), tok),
            pl.BlockSpec((1, 6, d), lambda bi, i: (bi, 0, 0)),
            pl.BlockSpec((1, bpt, Q_WIDTH, 128), lambda bi, i: (bi, i, 0, 0)),
            pl.BlockSpec((1, BLOCK, KV_WIDTH), lambda bi, i: (bi, prev(bi, i), 0)),
            pl.BlockSpec((1, tile, KV_WIDTH), tok),
            pl.BlockSpec((1, BLOCK, KV_WIDTH), lambda bi, i: (bi, nxt(bi, i), 0)),
            pl.BlockSpec((1, KV_WIDTH, BLOCK), lambda bi, i: (bi, 0, prev(bi, i))),
            pl.BlockSpec((1, KV_WIDTH, tile), lambda bi, i: (bi, 0, i)),
            pl.BlockSpec((1, KV_WIDTH, BLOCK), lambda bi, i: (bi, 0, nxt(bi, i))),
            pl.BlockSpec(bias.shape, lambda bi, i: (0, 0, 0, 0)),
            pl.BlockSpec(sink_tbl.shape, lambda bi, i: (0, 0, 0, 0)),
            pl.BlockSpec((Q_WIDTH, d), lambda bi, i: (0, 0)),
        ],
        out_specs=pl.BlockSpec((1, tile, d), tok),
        out_shape=jax.ShapeDtypeStruct(x.shape, F32),
        scratch_shapes=[
            pltpu.VMEM((tile + 2 * BLOCK, KV_WIDTH), BF16),
            pltpu.VMEM((bpt + 2, KV_WIDTH, BLOCK), BF16),
            pltpu.VMEM((tile, Q_WIDTH), F32),
        ],
        compiler_params=_params("parallel", "parallel"),
        name="window_attention",
    )(x, mod, q, k, k, k, vt, vt, vt, bias, sink_tbl, w_o[_HEAD_COLS].astype(BF16))


def _dft_tables(seq):
    n1 = BLOCK
    n2 = seq // n1
    k1 = np.arange(n1, dtype=np.int64)
    s1 = np.arange(n1, dtype=np.int64)
    s2 = np.arange(n2, dtype=np.int64)
    pos = s1[None, None, :] * n2 + s2[:, None, None]
    ang = 2.0 * np.pi * ((k1[None, :, None] * pos) % seq).astype(np.float64) / seq
    stage1 = np.concatenate([np.cos(ang), -np.sin(ang)], axis=1).astype(np.float32)
    k2 = np.arange(n2, dtype=np.int64)
    ang2 = 2.0 * np.pi * ((k2[:, None] * s2[None, :]) % n2).astype(np.float64) / n2
    c2, sn2 = np.cos(ang2), np.sin(ang2)
    stage2 = np.block([[c2, sn2], [-sn2, c2]]).astype(np.float32)
    stage2 = stage2.reshape(2 * n2, 2, n2).transpose(0, 2, 1).reshape(2 * n2, 2 * n2)
    c = np.arange(C_GROUP_DIM, dtype=np.int64)
    angc = 2.0 * np.pi * ((c[:, None] * c[None, :]) % C_GROUP_DIM).astype(np.float64) / C_GROUP_DIM
    chan = (np.concatenate([np.cos(angc), np.sin(angc)], axis=0) / np.sqrt(float(seq) * C_GROUP_DIM)).astype(np.float32)
    return n1, n2, stage1, stage2, chan


def _fourier1_kernel(x_ref, mod_ref, g_ref, win_ref, dft_ref, mid_ref, *, n1, grp):
    mod = mod_ref[0]
    x = x_ref[0].reshape(n1 * grp, D_MODEL)
    h = _norm_mod(x, g_ref[...], mod[1:2], mod[0:1]).astype(BF16)
    z = jnp.dot(h, win_ref[...], preferred_element_type=F32)
    z = _regroup_rows(z, n1).astype(BF16)
    y = jnp.concatenate([jnp.dot(dft_ref[g], z[g * n1:(g + 1) * n1], preferred_element_type=F32)
                         for g in range(grp)], axis=0)
    mid_ref[0] = _regroup_rows(y, 2 * grp).astype(BF16).reshape(n1, 2 * grp, D_MODEL)


def _fourier2_kernel(x_ref, mod_ref, mid_ref, dft_ref, chan_ref, wout_ref, o_ref, re_ref, im_ref, f_ref, *, n2, grp):
    for kl in range(grp):
        spec = jnp.dot(dft_ref[...], mid_ref[0, kl], preferred_element_type=F32)
        re_ref[kl * n2:(kl + 1) * n2] = spec[:n2].astype(BF16)
        im_ref[kl * n2:(kl + 1) * n2] = spec[n2:].astype(BF16)
    for cg in range(C_GROUPS):
        cols = slice(cg * C_GROUP_DIM, (cg + 1) * C_GROUP_DIM)
        lhs = jnp.concatenate([re_ref[:, cols], im_ref[:, cols]], axis=1)
        f_ref[:, cols] = jnp.dot(lhs, chan_ref[...], preferred_element_type=F32).astype(BF16)
    m = jnp.dot(f_ref[...], wout_ref[...], preferred_element_type=F32)
    m = _regroup_rows(m, grp)
    out = x_ref[0].reshape(n2 * grp, D_MODEL) + mod_ref[0][2:3] * m
    o_ref[0] = out.reshape(n2, grp, D_MODEL)


def _fourier_layer(x, mod, gain, w_in, w_out):
    b, s, d = x.shape
    n1, n2, stage1, stage2, chan = _dft_tables(s)
    grp = HALO
    const2 = lambda bi, j: (0, 0)
    mid = pl.pallas_call(
        functools.partial(_fourier1_kernel, n1=n1, grp=grp),
        grid=(b, n2 // grp),
        in_specs=[
            pl.BlockSpec((1, n1, grp, d), lambda bi, j: (bi, 0, j, 0)),
            pl.BlockSpec((1, 6, d), lambda bi, j: (bi, 0, 0)),
            pl.BlockSpec((1, d), const2),
            pl.BlockSpec((d, d), const2),
            pl.BlockSpec((grp, 2 * n1, n1), lambda bi, j: (j, 0, 0)),
        ],
        out_specs=pl.BlockSpec((1, n1, 2 * grp, d), lambda bi, j: (bi, 0, j, 0)),
        out_shape=jax.ShapeDtypeStruct((b, n1, 2 * n2, d), BF16),
        compiler_params=_params("parallel", "parallel"),
        name="fourier_stage1",
    )(x.reshape(b, n1, n2, d), mod, gain.reshape(1, d), w_in.astype(BF16), jnp.asarray(stage1).astype(BF16))
    out = pl.pallas_call(
        functools.partial(_fourier2_kernel, n2=n2, grp=grp),
        grid=(b, n1 // grp),
        in_specs=[
            pl.BlockSpec((1, n2, grp, d), lambda bi, j: (bi, 0, j, 0)),
            pl.BlockSpec((1, 6, d), lambda bi, j: (bi, 0, 0)),
            pl.BlockSpec((1, grp, 2 * n2, d), lambda bi, j: (bi, j, 0, 0)),
            pl.BlockSpec((2 * n2, 2 * n2), const2),
            pl.BlockSpec((2 * C_GROUP_DIM, C_GROUP_DIM), const2),
            pl.BlockSpec((d, d), const2),
        ],
        out_specs=pl.BlockSpec((1, n2, grp, d), lambda bi, j: (bi, 0, j, 0)),
        out_shape=jax.ShapeDtypeStruct((b, n2, n1, d), F32),
        scratch_shapes=[pltpu.VMEM((grp * n2, d), BF16)] * 3,
        compiler_params=_params("parallel", "parallel"),
        name="fourier_stage2",
    )(x.reshape(b, n2, n1, d), mod, mid, jnp.asarray(stage2).astype(BF16),
      jnp.asarray(chan).astype(BF16), w_out.astype(BF16))
    return out.reshape(b, s, d)


def _ffn_kernel(x_ref, xph_ref, xnh_ref, modp_ref, mode_ref, g_ref, wup_ref, wc_ref, bc_ref, wdn_ref, gf_ref,
                o_ref, xs_ref, h_ref, up_ref, act_ref, down_ref, *, tile, tps, n_tiles, final):
    t = pl.program_id(0)
    gain = g_ref[...]
    n_pieces = D_MODEL // 128
    n_chunks = FF_DIM // FF_CHUNK
    assert n_chunks >= n_pieces + 3

    def prepare_piece(p):
        piece = _regroup_rows(x_ref[:, p * 128:(p + 1) * 128], HALO)
        xs_ref[t % 3, :, p * 128:(p + 1) * 128] = piece
        return piece * piece

    def prepare_finish(sq, part):
        slot = t % 2
        mod = modp_ref[0]
        scale, shift = mod[4:5], mod[3:4]
        if part < 2:
            rows = slice(part * tile // 2, (part + 1) * tile // 2)
            inv = lax.rsqrt(jnp.sum(sq[rows], axis=-1, keepdims=True) * (1.0 / D_MODEL) + EPS)
            hm = xs_ref[t % 3, rows] * inv * (gain * (1.0 + scale)) + shift
            h_ref[slot, rows] = hm.astype(BF16)
            return hm[:, :128]
        pos = jnp.minimum(t, n_tiles - 1) % tps
        h_next = _norm_mod(xnh_ref[...], gain, scale, shift) * (pos < tps - 1).astype(F32)
        h_prev = _norm_mod(xph_ref[...], gain, scale, shift) * (pos > 0).astype(F32)
        hh = jnp.concatenate([h_next, h_prev], axis=0)
        h_ref[slot, tile:] = hh.astype(BF16)
        return hh[:, :128]

    def finish_piece(p):
        cols = slice(p * 128, (p + 1) * 128)
        y = _regroup_rows(xs_ref[(t + 1) % 3, :, cols] + mode_ref[0][5:6, cols] * down_ref[:, cols], tile // HALO)
        o_ref[:, cols] = y
        return y, (jnp.sum(y * y, axis=-1, keepdims=True) if final else None)

    def finish_scale(ysq, part):
        rows = slice(part * tile // 2, (part + 1) * tile // 2)
        scaled = o_ref[rows] * lax.rsqrt(ysq[rows] * (1.0 / D_MODEL) + EPS) * gf_ref[...]
        o_ref[rows] = scaled
        return scaled[:, :128]

    def add(a, b):
        return b if a is None else a + b

    @pl.when(t == 0)
    def _():
        down_ref[...] = jnp.zeros_like(down_ref)
        xs_ref[2] = jnp.zeros_like(down_ref)
        sq = None
        for p in range(n_pieces):
            sq = add(sq, prepare_piece(p))
        for part in range(3):
            prepare_finish(sq, part)

    @pl.when(t == n_tiles + 1)
    def _():
        ysq = None
        for p in range(n_pieces):
            ysq = add(ysq, finish_piece(p)[1])
        if final:
            for part in range(2):
                finish_scale(ysq, part)

    @pl.when(jnp.logical_and(t >= 1, t <= n_tiles))
    def _():
        h = h_ref[(t + 1) % 2]
        sub = lax.broadcasted_iota(jnp.int32, (HALO, FF_CHUNK), 0)

        def up_proj(j):
            for half in range(2):
                c0 = half * FF_DIM + j * FF_CHUNK
                up = jnp.dot(h, wup_ref[:, c0:c0 + FF_CHUNK], preferred_element_type=F32)
                dst = up_ref.at[j % 2, half]
                dst[HALO:HALO + tile] = up[:tile]
                dst[0:HALO] = jnp.where(sub == 0, pltpu.roll(up[tile + HALO:], 1, 0),
                                        pltpu.roll(up[tile - HALO:tile], 1, 0))
                dst[HALO + tile:] = jnp.where(sub == HALO - 1, pltpu.roll(up[tile:tile + HALO], HALO - 1, 0),
                                              pltpu.roll(up[0:HALO], HALO - 1, 0))

        def conv(j, half):
            cols = slice(half * FF_DIM + j * FF_CHUNK, half * FF_DIM + (j + 1) * FF_CHUNK)
            w = wc_ref[:, cols]
            src = up_ref.at[j % 2, half]
            return (src[0:tile] * w[0:1] + src[HALO:HALO + tile] * w[1:2] + src[2 * HALO:2 * HALO + tile] * w[2:3]
                    + bc_ref[:, cols])

        def tied(a, tie, row0):
            r = tie.shape[0]
            mid = a[row0:row0 + r]
            blocks = [a[:row0]] if row0 else []
            blocks.append(jnp.concatenate([mid[:, :128] + tie, mid[:, 128:]], axis=1))
            if row0 + r < tile:
                blocks.append(a[row0 + r:])
            return jnp.concatenate(blocks, axis=0)

        up_proj(0)
        sq = None
        ysq = None
        never = t < 0
        for j in range(n_chunks):
            if j + 1 < n_chunks:
                up_proj(j + 1)
            a = conv(j, 0)
            if j < n_pieces:
                piece_sq = prepare_piece(j)
                sq = add(sq, piece_sq)
                y, y_sq = finish_piece(j)
                ysq = add(ysq, y_sq)
                a = tied(a, jnp.where(never, piece_sq + y, 0.0), 0)
            else:
                part = j - n_pieces
                a = tied(a, jnp.where(never, prepare_finish(sq, part), 0.0), (part % 2) * tile // 2)
                if final and part < 2:
                    a = tied(a, jnp.where(never, finish_scale(ysq, part), 0.0), part * tile // 2)
            g = conv(j, 1)
            act_ref[:, j * FF_CHUNK:(j + 1) * FF_CHUNK] = (a * (g * jax.nn.sigmoid(g))).astype(BF16)
        down_ref[...] = jnp.dot(act_ref[...], wdn_ref[...], preferred_element_type=F32)


def _ffn_layer(x, mod, gain, w_up_all, w_conv, b_conv, w_down_all, g_final, layer, final):
    b, s, d = x.shape
    tile = min(TOKEN_TILE, s)
    tps = s // tile
    n_tiles = b * tps
    hpt = tile // HALO
    n_halo = b * s // HALO
    const2 = lambda t: (0, 0)
    cur = lambda t: jnp.minimum(t, n_tiles - 1)
    done = lambda t: jnp.clip(t - 2, 0, n_tiles - 1)
    x2 = x.reshape(b * s, d)
    out = pl.pallas_call(
        functools.partial(_ffn_kernel, tile=tile, tps=tps, n_tiles=n_tiles, final=final),
        grid=(n_tiles + 2,),
        in_specs=[
            pl.BlockSpec((tile, d), lambda t: (cur(t), 0)),
            pl.BlockSpec((HALO, d), lambda t: (jnp.maximum(cur(t) * hpt - 1, 0), 0)),
            pl.BlockSpec((HALO, d), lambda t: (jnp.minimum((cur(t) + 1) * hpt, n_halo - 1), 0)),
            pl.BlockSpec((1, 6, d), lambda t: (cur(t) // tps, 0, 0)),
            pl.BlockSpec((1, 6, d), lambda t: (done(t) // tps, 0, 0)),
            pl.BlockSpec((1, d), const2),
            pl.BlockSpec((None, d, 2 * FF_DIM), lambda t: (layer, 0, 0)),
            pl.BlockSpec((3, 2 * FF_DIM), const2),
            pl.BlockSpec((1, 2 * FF_DIM), const2),
            pl.BlockSpec((None, FF_DIM, d), lambda t: (layer, 0, 0)),
            pl.BlockSpec((1, d), const2),
        ],
        out_specs=pl.BlockSpec((tile, d), lambda t: (done(t), 0)),
        out_shape=jax.ShapeDtypeStruct((b * s, d), F32),
        scratch_shapes=[
            pltpu.VMEM((3, tile, d), F32),
            pltpu.VMEM((2, tile + 2 * HALO, d), BF16),
            pltpu.VMEM((2, 2, tile + 2 * HALO, FF_CHUNK), F32),
            pltpu.VMEM((tile, FF_DIM), BF16),
            pltpu.VMEM((tile, d), F32),
        ],
        compiler_params=_params("arbitrary"),
        name="conv_ffn",
    )(x2, x2, x2, mod, mod, gain.reshape(1, d), w_up_all, w_conv, b_conv.reshape(1, 2 * FF_DIM), w_down_all,
      g_final.reshape(1, d))
    return out.reshape(b, s, d)


def _trunk(x, mods, norm_g, a_w_in, a_g_v, a_w_s, a_b_s, a_w_out, b_w_qkv, b_sinks, b_w_o, c_w_in, c_w_out,
           f_w_up, f_w_conv, f_b_conv, f_w_down, g_final):
    for i in range(DEPTH):
        mod = mods[i]
        kind, j = i % N_MIXERS, i // N_MIXERS
        if kind == 0:
            x = _gmlp_layer(x, mod, norm_g[i, 0], a_w_in[j], a_g_v[j], a_w_s[j], a_b_s[j], a_w_out[j])
        elif kind == 1:
            q, k, v = _qkv_proj(x, mod, norm_g[i, 0], b_w_qkv[j])
            x = _attn_layer(x, mod, q, k, v, b_sinks[j], b_w_o[j])
        else:
            x = _fourier_layer(x, mod, norm_g[i, 0], c_w_in[j], c_w_out[j])
        x = _ffn_layer(x, mod, norm_g[i, 1], f_w_up, f_w_conv[i], f_b_conv[i], f_w_down, g_final, layer=i,
                       final=(i == DEPTH - 1))
    return x


def kernel(x_prompt, x_sample, c_prompt, c_sample, w_ada, b_ada, norm_g, a_w_in, a_g_v, a_w_s, a_b_s, a_w_out,
           b_w_qkv, b_sinks, b_w_o, c_w_in, c_w_out, f_w_up, f_w_conv, f_b_conv, f_w_down, g_final):
    nb_p, nb_s = x_prompt.shape[0], x_sample.shape[0]
    assert nb_p + nb_s <= MOD_ROWS
    c_all = jnp.concatenate(
        [c_prompt, c_sample, jnp.zeros((MOD_ROWS - nb_p - nb_s, D_MODEL), F32)], axis=0)
    mods = _modulation(c_all, w_ada, b_ada).reshape(DEPTH, MOD_ROWS, 6, D_MODEL)
    weights = (norm_g, a_w_in, a_g_v, a_w_s, a_b_s, a_w_out, b_w_qkv, b_sinks, b_w_o, c_w_in, c_w_out,
               f_w_up.astype(BF16), f_w_conv, f_b_conv, f_w_down.astype(BF16), g_final)
    y_prompt = _trunk(x_prompt, mods[:, :nb_p], *weights)
    y_sample = _trunk(x_sample, mods[:, nb_p:nb_p + nb_s], *weights)
    return (y_prompt, y_sample)
```

```python
import functools

import numpy as np
import jax
import jax.numpy as jnp
from jax import lax
from jax.experimental import pallas as pl
from jax.experimental.pallas import tpu as pltpu

F32 = jnp.float32
BF16 = jnp.bfloat16

D_MODEL = 1024
DEPTH = 4
N_MIXERS = 3
EPS = 1e-6
NEG_INF = -1e30
CHUNK = 128
A_GROUPS = 8
A_HEAD = D_MODEL // A_GROUPS
B_HEADS = 16
B_KV_HEADS = 4
B_HEAD_DIM = 64
B_Q_PER_KV = B_HEADS // B_KV_HEADS
WINDOW = 128
BLOCK = 128
Q_WIDTH = B_HEADS * B_HEAD_DIM
KV_WIDTH = B_KV_HEADS * B_HEAD_DIM
C_GROUPS = 8
C_GROUP_DIM = D_MODEL // C_GROUPS
FF_DIM = 2816
FF_CHUNK = 256
HALO = 8

MOD_ROWS = 8
MOD_NT = 1536
VMEM_LIMIT = 56 * 1024 * 1024

TOKEN_TILE = 512


def _params(*sem):
    return pltpu.CompilerParams(dimension_semantics=sem, vmem_limit_bytes=VMEM_LIMIT)


def _norm_mod(x, gain, scale, shift):
    ms = jnp.mean(x * x, axis=-1, keepdims=True)
    return x * lax.rsqrt(ms + EPS) * (gain * (1.0 + scale)) + shift


def _regroup_rows(x, outer):
    rows, cols = x.shape
    return jnp.transpose(x.reshape(outer, rows // outer, cols), (1, 0, 2)).reshape(rows, cols)


def _mod_kernel(c_ref, w_ref, b_ref, o_ref):
    c = c_ref[...]
    cs = c * jax.nn.sigmoid(c)
    o_ref[0] = jnp.dot(cs, w_ref[0], preferred_element_type=F32) + b_ref[0]


def _modulation(c_all, w_ada, b_ada):
    n_out = w_ada.shape[-1]
    return pl.pallas_call(
        _mod_kernel,
        grid=(DEPTH, n_out // MOD_NT),
        in_specs=[
            pl.BlockSpec((MOD_ROWS, D_MODEL), lambda i, j: (0, 0)),
            pl.BlockSpec((1, D_MODEL, MOD_NT), lambda i, j: (i, 0, j)),
            pl.BlockSpec((1, 1, MOD_NT), lambda i, j: (i, 0, j)),
        ],
        out_specs=pl.BlockSpec((1, MOD_ROWS, MOD_NT), lambda i, j: (i, 0, j)),
        out_shape=jax.ShapeDtypeStruct((DEPTH, MOD_ROWS, n_out), F32),
        compiler_params=_params("parallel", "parallel"),
        name="modulation",
    )(c_all, w_ada, b_ada.reshape(DEPTH, 1, n_out))


def _gmlp_kernel(x_ref, mod_ref, g_ref, win_ref, gv_ref, ws_ref, bs_ref, wout_ref, o_ref, gated_ref, *, tile):
    mod = mod_ref[0]
    half = tile // 2
    n_sub = half // CHUNK
    halves = [slice(i * half, (i + 1) * half) for i in range(2)]
    hs = [_norm_mod(x_ref[0, r, :], g_ref[...], mod[1:2], mod[0:1]).astype(BF16) for r in halves]
    uvs = [jnp.dot(h, win_ref[...], preferred_element_type=F32) for h in hs]
    for i, r in enumerate(halves):
        uv = uvs[i]
        uv = 0.5 * uv * (1.0 + lax.erf(uv * np.float32(1.0 / np.sqrt(2.0))))
        u = uv[:, :D_MODEL]
        v = uv[:, D_MODEL:]
        v = v * lax.rsqrt(jnp.mean(v * v, axis=-1, keepdims=True) + EPS) * gv_ref[...]
        vb = v.astype(BF16)
        for g in range(A_GROUPS):
            cols = slice(g * A_HEAD, (g + 1) * A_HEAD)
            rhs = jnp.concatenate([vb[c * CHUNK:(c + 1) * CHUNK, cols] for c in range(n_sub)], axis=1)
            sv = jnp.dot(ws_ref[g], rhs, preferred_element_type=F32)
            for c in range(n_sub):
                rows = slice(c * CHUNK, (c + 1) * CHUNK)
                gated_ref[i, rows, cols] = (u[rows, cols] * (sv[:, c * CHUNK:(c + 1) * CHUNK] + bs_ref[g])).astype(BF16)
        m = jnp.dot(gated_ref[i], wout_ref[...], preferred_element_type=F32)
        o_ref[0, r, :] = x_ref[0, r, :] + mod[2:3] * m


def _gmlp_layer(x, mod, gain, w_in, g_v, w_s, b_s, w_out):
    b, s, d = x.shape
    tile = min(2 * TOKEN_TILE, s)
    bs_full = jnp.broadcast_to(b_s[:, :, None], (A_GROUPS, CHUNK, A_HEAD))
    const2 = lambda bi, i: (0, 0)
    const3 = lambda bi, i: (0, 0, 0)
    return pl.pallas_call(
        functools.partial(_gmlp_kernel, tile=tile),
        grid=(b, s // tile),
        in_specs=[
            pl.BlockSpec((1, tile, d), lambda bi, i: (bi, i, 0)),
            pl.BlockSpec((1, 6, d), lambda bi, i: (bi, 0, 0)),
            pl.BlockSpec((1, d), const2),
            pl.BlockSpec((d, 2 * d), const2),
            pl.BlockSpec((1, d), const2),
            pl.BlockSpec((A_GROUPS, CHUNK, CHUNK), const3),
            pl.BlockSpec((A_GROUPS, CHUNK, A_HEAD), const3),
            pl.BlockSpec((d, d), const2),
        ],
        out_specs=pl.BlockSpec((1, tile, d), lambda bi, i: (bi, i, 0)),
        out_shape=jax.ShapeDtypeStruct(x.shape, F32),
        scratch_shapes=[pltpu.VMEM((2, tile // 2, d), BF16)],
        compiler_params=_params("parallel", "parallel"),
        name="gmlp_mixer",
    )(x, mod, gain.reshape(1, d), w_in.astype(BF16), g_v.reshape(1, d), w_s.astype(BF16), bs_full,
      w_out.astype(BF16))


_PAIRS = B_KV_HEADS // 2
_HEAD_ORDER = [((2 * (t // B_Q_PER_KV) + half) * B_Q_PER_KV + t % B_Q_PER_KV)
               for t in range(B_HEADS // 2) for half in range(2)]
_HEAD_COLS = np.concatenate([np.arange(h * B_HEAD_DIM, (h + 1) * B_HEAD_DIM) for h in _HEAD_ORDER])
GROUP_ROWS = B_Q_PER_KV * BLOCK
BAND = 3 * BLOCK
PAIR_WIDTH = 2 * B_HEAD_DIM
LOG2E = float(np.log2(np.e))
_NT = (((1,), (1,)), ((), ()))


def _qkv_kernel(x_ref, mod_ref, g_ref, wqk_ref, wvt_ref, q_ref, k_ref, vt_ref, *, tile):
    mod = mod_ref[0]
    h = _norm_mod(x_ref[0], g_ref[...], mod[1:2], mod[0:1]).astype(BF16)
    qk = jnp.dot(h, wqk_ref[...], preferred_element_type=F32)
    for blk in range(tile // BLOCK):
        rows = slice(blk * BLOCK, (blk + 1) * BLOCK)
        for t in range(Q_WIDTH // PAIR_WIDTH):
            q_ref[0, blk, t * BLOCK:(t + 1) * BLOCK, :] = (
                qk[rows, t * PAIR_WIDTH:(t + 1) * PAIR_WIDTH] * (B_HEAD_DIM ** -0.5 * LOG2E)).astype(BF16)
    k_ref[0] = qk[:, Q_WIDTH:].astype(BF16)
    vt_ref[0] = lax.dot_general(wvt_ref[...], h, _NT, preferred_element_type=F32).astype(BF16)


def _qkv_proj(x, mod, gain, w_qkv):
    b, s, d = x.shape
    tile = min(2 * TOKEN_TILE, s)
    tok = lambda bi, i: (bi, i, 0)
    const2 = lambda bi, i: (0, 0)
    w_qk = jnp.concatenate([w_qkv[:, _HEAD_COLS], w_qkv[:, Q_WIDTH:Q_WIDTH + KV_WIDTH]], axis=1).astype(BF16)
    w_vt = w_qkv[:, Q_WIDTH + KV_WIDTH:].T.astype(BF16)
    return pl.pallas_call(
        functools.partial(_qkv_kernel, tile=tile),
        grid=(b, s // tile),
        in_specs=[
            pl.BlockSpec((1, tile, d), tok),
            pl.BlockSpec((1, 6, d), lambda bi, i: (bi, 0, 0)),
            pl.BlockSpec((1, d), const2),
            pl.BlockSpec((d, Q_WIDTH + KV_WIDTH), const2),
            pl.BlockSpec((KV_WIDTH, d), const2),
        ],
        out_specs=[
            pl.BlockSpec((1, tile // BLOCK, Q_WIDTH, PAIR_WIDTH), lambda bi, i: (bi, i, 0, 0)),
            pl.BlockSpec((1, tile, KV_WIDTH), tok),
            pl.BlockSpec((1, KV_WIDTH, tile), lambda bi, i: (bi, 0, i)),
        ],
        out_shape=[
            jax.ShapeDtypeStruct((b, s // BLOCK, Q_WIDTH, PAIR_WIDTH), BF16),
            jax.ShapeDtypeStruct((b, s, KV_WIDTH), BF16),
            jax.ShapeDtypeStruct((b, KV_WIDTH, s), BF16),
        ],
        compiler_params=_params("parallel", "parallel"),
        name="qkv_proj",
    )(x, mod, gain.reshape(1, d), w_qk, w_vt)


def _attn_bias_table():
    qq = np.arange(BLOCK)[None, :]
    kk = np.arange(BAND)[:, None]
    dist = np.abs(qq + BLOCK - kk)
    slopes = np.exp2(np.float32(-8.0) * np.arange(1, B_HEADS + 1, dtype=np.float32) / np.float32(B_HEADS))
    tbl = np.empty((3, _PAIRS, 2 * BAND, GROUP_ROWS), np.float32)
    for case in range(3):
        valid = dist <= WINDOW
        if case == 0:
            valid = valid & (kk >= BLOCK)
        if case == 2:
            valid = valid & (kk < 2 * BLOCK)
        for p in range(_PAIRS):
            for c in range(B_Q_PER_KV):
                for half in range(2):
                    hd = (2 * p + half) * B_Q_PER_KV + c
                    alibi = (-slopes[hd].astype(np.float64) * dist * LOG2E).astype(np.float32)
                    tbl[case, p, half * BAND:(half + 1) * BAND, c * BLOCK:(c + 1) * BLOCK] = np.where(
                        valid, alibi, np.float32(NEG_INF))
    return tbl


def _attn_kernel(x_ref, mod_ref, q_ref, kp_ref, k_ref, kn_ref, vp_ref, v_ref, vn_ref, bias_ref, sink_ref, wo_ref,
                 o_ref, kext_ref, vext_ref, ocat_ref, *, tile, nblk):
    i = pl.program_id(1)
    bpt = tile // BLOCK
    kext_ref[0:BLOCK] = kp_ref[0]
    kext_ref[BLOCK:BLOCK + tile] = k_ref[0]
    kext_ref[BLOCK + tile:] = kn_ref[0]
    vext_ref[0] = vp_ref[0]
    for j in range(bpt):
        vext_ref[1 + j] = v_ref[0, :, j * BLOCK:(j + 1) * BLOCK]
    vext_ref[1 + bpt] = vn_ref[0]
    low_lane = lax.broadcasted_iota(jnp.int32, (BAND, PAIR_WIDTH), 1) < B_HEAD_DIM
    low_row = lax.broadcasted_iota(jnp.int32, (PAIR_WIDTH, BAND), 0) < B_HEAD_DIM
    low_out = lax.broadcasted_iota(jnp.int32, (PAIR_WIDTH, GROUP_ROWS), 0) < B_HEAD_DIM
    ones_row = lax.broadcasted_iota(jnp.int32, (2 * HALO, 2 * BAND), 0) < HALO
    first_half = lax.broadcasted_iota(jnp.int32, (2 * HALO, 2 * BAND), 1) < BAND
    ones = jnp.where(ones_row == first_half, 1.0, 0.0).astype(BF16)

    def scores(qb, p):
        rows = slice(p * PAIR_WIDTH, (p + 1) * PAIR_WIDTH)
        n = i * bpt + qb
        case = jnp.where(n == 0, 0, jnp.where(n == nblk - 1, 2, 1))
        kb = kext_ref[qb * BLOCK:qb * BLOCK + BAND, rows]
        zk = jnp.zeros_like(kb)
        kz = jnp.concatenate([jnp.where(low_lane, kb, zk), jnp.where(low_lane, zk, kb)], axis=0)
        q4 = q_ref[0, qb, p * GROUP_ROWS:(p + 1) * GROUP_ROWS, :]
        return lax.dot_general(kz, q4, _NT, preferred_element_type=F32) + bias_ref[case, p]

    def attend(qb, p, sc):
        rows = slice(p * PAIR_WIDTH, (p + 1) * PAIR_WIDTH)
        vt = jnp.concatenate([vext_ref[qb, rows, :], vext_ref[qb + 1, rows, :], vext_ref[qb + 2, rows, :]], axis=1)
        zv = jnp.zeros_like(vt)
        vz = jnp.concatenate([
            jnp.concatenate([jnp.where(low_row, vt, zv), jnp.where(low_row, zv, vt)], axis=1), ones], axis=0)
        probs = []
        shift = []
        for half in range(2):
            sh = sc[half * BAND:(half + 1) * BAND]
            sink = sink_ref[p, half]
            mx = jnp.maximum(jnp.max(sh, axis=0, keepdims=True), sink)
            probs.append(jnp.exp2(sh - mx).astype(BF16))
            shift.append(jnp.exp2(sink - mx))
        pv = jnp.dot(vz, jnp.concatenate(probs, axis=0), preferred_element_type=F32)
        inv_a = 1.0 / (pv[PAIR_WIDTH:PAIR_WIDTH + 1] + shift[0])
        inv_b = 1.0 / (pv[PAIR_WIDTH + HALO:PAIR_WIDTH + HALO + 1] + shift[1])
        on = pv[:PAIR_WIDTH] * jnp.where(low_out, inv_a, inv_b)
        for c in range(B_Q_PER_KV):
            t = p * B_Q_PER_KV + c
            ocat_ref[qb * BLOCK:(qb + 1) * BLOCK, t * PAIR_WIDTH:(t + 1) * PAIR_WIDTH] = on[:, c * BLOCK:(c + 1) * BLOCK].T

    units = [(qb, p) for qb in range(bpt) for p in range(_PAIRS)]
    sc = scores(*units[0])
    for u, unit in enumerate(units):
        sc_next = scores(*units[u + 1]) if u + 1 < len(units) else None
        attend(*unit, sc)
        sc = sc_next
    m = jnp.dot(ocat_ref[...].astype(BF16), wo_ref[...], preferred_element_type=F32)
    o_ref[0] = x_ref[0] + mod_ref[0][2:3] * m


def _attn_layer(x, mod, q, k, vt, sinks, w_o):
    b, s, d = x.shape
    tile = min(2 * TOKEN_TILE, s)
    bpt = tile // BLOCK
    nblk = s // BLOCK
    assert nblk >= 2
    tok = lambda bi, i: (bi, i, 0)
    prev = lambda bi, i: jnp.maximum(i * bpt - 1, 0)
    nxt = lambda bi, i: jnp.minimum((i + 1) * bpt, nblk - 1)
    bias = jnp.asarray(_attn_bias_table())
    head = np.array([[[(2 * p + half) * B_Q_PER_KV + c for c in range(B_Q_PER_KV)] for half in range(2)]
                     for p in range(_PAIRS)])
    sink_tbl = jnp.repeat(sinks.astype(F32)[head] * LOG2E, BLOCK, axis=-1)[:, :, None, :]
    return pl.pallas_call(
        functools.partial(_attn_kernel, tile=tile, nblk=nblk),
        grid=(b, s // tile),
        in_specs=[
            pl.BlockSpec((1, tile, d), tok),
            pl.BlockSpec((1, 6, d), lambda bi, i: (bi, 0, 0)),
            pl.BlockSpec((1, bpt, Q_WIDTH, PAIR_WIDTH), lambda bi, i: (bi, i, 0, 0)),
            pl.BlockSpec((1, BLOCK, KV_WIDTH), lambda bi, i: (bi, prev(bi, i), 0)),
            pl.BlockSpec((1, tile, KV_WIDTH), tok),
            pl.BlockSpec((1, BLOCK, KV_WIDTH), lambda bi, i: (bi, nxt(bi, i), 0)),
            pl.BlockSpec((1, KV_WIDTH, BLOCK), lambda bi, i: (bi, 0, prev(bi, i))),
            pl.BlockSpec((1, KV_WIDTH, tile), lambda bi, i: (bi, 0, i)),
            pl.BlockSpec((1, KV_WIDTH, BLOCK), lambda bi, i: (bi, 0, nxt(bi, i))),
            pl.BlockSpec(bias.shape, lambda bi, i: (0, 0, 0, 0)),
            pl.BlockSpec(sink_tbl.shape, lambda bi, i: (0, 0, 0, 0)),
            pl.BlockSpec((Q_WIDTH, d), lambda bi, i: (0, 0)),
        ],
        out_specs=pl.BlockSpec((1, tile, d), tok),
        out_shape=jax.ShapeDtypeStruct(x.shape, F32),
        scratch_shapes=[
            pltpu.VMEM((tile + 2 * BLOCK, KV_WIDTH), BF16),
            pltpu.VMEM((bpt + 2, KV_WIDTH, BLOCK), BF16),
            pltpu.VMEM((tile, Q_WIDTH), F32),
        ],
        compiler_params=_params("parallel", "parallel"),
        name="window_attention",
    )(x, mod, q, k, k, k, vt, vt, vt, bias, sink_tbl, w_o[_HEAD_COLS].astype(BF16))


def _dft_tables(seq):
    n1 = BLOCK
    n2 = seq // n1
    k1 = np.arange(n1, dtype=np.int64)
    s1 = np.arange(n1, dtype=np.int64)
    s2 = np.arange(n2, dtype=np.int64)
    pos = s1[None, None, :] * n2 + s2[:, None, None]
    ang = 2.0 * np.pi * ((k1[None, :, None] * pos) % seq).astype(np.float64) / seq
    stage1 = np.concatenate([np.cos(ang), -np.sin(ang)], axis=1).astype(np.float32)
    k2 = np.arange(n2, dtype=np.int64)
    ang2 = 2.0 * np.pi * ((k2[:, None] * s2[None, :]) % n2).astype(np.float64) / n2
    c2, sn2 = np.cos(ang2), np.sin(ang2)
    stage2 = np.block([[c2, sn2], [-sn2, c2]]).astype(np.float32)
    stage2 = stage2.reshape(2 * n2, 2, n2).transpose(0, 2, 1).reshape(2 * n2, 2 * n2)
    c = np.arange(C_GROUP_DIM, dtype=np.int64)
    angc = 2.0 * np.pi * ((c[:, None] * c[None, :]) % C_GROUP_DIM).astype(np.float64) / C_GROUP_DIM
    chan = (np.concatenate([np.cos(angc), np.sin(angc)], axis=0) / np.sqrt(float(seq) * C_GROUP_DIM)).astype(np.float32)
    return n1, n2, stage1, stage2, chan


def _fourier1_kernel(x_ref, mod_ref, g_ref, win_ref, dft_ref, mid_ref, *, n1, grp):
    mod = mod_ref[0]
    x = x_ref[0].reshape(n1 * grp, D_MODEL)
    h = _norm_mod(x, g_ref[...], mod[1:2], mod[0:1]).astype(BF16)
    z = jnp.dot(h, win_ref[...], preferred_element_type=F32)
    z = _regroup_rows(z, n1).astype(BF16)
    y = jnp.concatenate([jnp.dot(dft_ref[g], z[g * n1:(g + 1) * n1], preferred_element_type=F32)
                         for g in range(grp)], axis=0)
    mid_ref[0] = _regroup_rows(y, 2 * grp).astype(BF16).reshape(n1, 2 * grp, D_MODEL)


def _fourier2_kernel(x_ref, mod_ref, mid_ref, dft_ref, chan_ref, wout_ref, o_ref, re_ref, im_ref, f_ref, *, n2, grp):
    for kl in range(grp):
        spec = jnp.dot(dft_ref[...], mid_ref[0, kl], preferred_element_type=F32)
        re_ref[kl * n2:(kl + 1) * n2] = spec[:n2].astype(BF16)
        im_ref[kl * n2:(kl + 1) * n2] = spec[n2:].astype(BF16)
    for cg in range(C_GROUPS):
        cols = slice(cg * C_GROUP_DIM, (cg + 1) * C_GROUP_DIM)
        lhs = jnp.concatenate([re_ref[:, cols], im_ref[:, cols]], axis=1)
        f_ref[:, cols] = jnp.dot(lhs, chan_ref[...], preferred_element_type=F32).astype(BF16)
    m = jnp.dot(f_ref[...], wout_ref[...], preferred_element_type=F32)
    m = _regroup_rows(m, grp)
    out = x_ref[0].reshape(n2 * grp, D_MODEL) + mod_ref[0][2:3] * m
    o_ref[0] = out.reshape(n2, grp, D_MODEL)


def _fourier_layer(x, mod, gain, w_in, w_out):
    b, s, d = x.shape
    n1, n2, stage1, stage2, chan = _dft_tables(s)
    grp = HALO
    const2 = lambda bi, j: (0, 0)
    mid = pl.pallas_call(
        functools.partial(_fourier1_kernel, n1=n1, grp=grp),
        grid=(b, n2 // grp),
        in_specs=[
            pl.BlockSpec((1, n1, grp, d), lambda bi, j: (bi, 0, j, 0)),
            pl.BlockSpec((1, 6, d), lambda bi, j: (bi, 0, 0)),
            pl.BlockSpec((1, d), const2),
            pl.BlockSpec((d, d), const2),
            pl.BlockSpec((grp, 2 * n1, n1), lambda bi, j: (j, 0, 0)),
        ],
        out_specs=pl.BlockSpec((1, n1, 2 * grp, d), lambda bi, j: (bi, 0, j, 0)),
        out_shape=jax.ShapeDtypeStruct((b, n1, 2 * n2, d), BF16),
        compiler_params=_params("parallel", "parallel"),
        name="fourier_stage1",
    )(x.reshape(b, n1, n2, d), mod, gain.reshape(1, d), w_in.astype(BF16), jnp.asarray(stage1).astype(BF16))
    out = pl.pallas_call(
        functools.partial(_fourier2_kernel, n2=n2, grp=grp),
        grid=(b, n1 // grp),
        in_specs=[
            pl.BlockSpec((1, n2, grp, d), lambda bi, j: (bi, 0, j, 0)),
            pl.BlockSpec((1, 6, d), lambda bi, j: (bi, 0, 0)),
            pl.BlockSpec((1, grp, 2 * n2, d), lambda bi, j: (bi, j, 0, 0)),
            pl.BlockSpec((2 * n2, 2 * n2), const2),
            pl.BlockSpec((2 * C_GROUP_DIM, C_GROUP_DIM), const2),
            pl.BlockSpec((d, d), const2),
        ],
        out_specs=pl.BlockSpec((1, n2, grp, d), lambda bi, j: (bi, 0, j, 0)),
        out_shape=jax.ShapeDtypeStruct((b, n2, n1, d), F32),
        scratch_shapes=[pltpu.VMEM((grp * n2, d), BF16)] * 3,
        compiler_params=_params("parallel", "parallel"),
        name="fourier_stage2",
    )(x.reshape(b, n2, n1, d), mod, mid, jnp.asarray(stage2).astype(BF16),
      jnp.asarray(chan).astype(BF16), w_out.astype(BF16))
    return out.reshape(b, s, d)


def _ffn_kernel(xp_ref, x_ref, xn_ref, mod_ref, g_ref, wup_ref, wc_ref, bc_ref, wdn_ref, gf_ref, o_ref,
                h_ref, up_ref, act_ref, *, tile, final):
    i = pl.program_id(1)
    last = pl.num_programs(1) - 1
    mod = mod_ref[0]
    gain = g_ref[...]
    x = _regroup_rows(x_ref[0], HALO)
    h_ref[0:tile] = _norm_mod(x, gain, mod[4:5], mod[3:4]).astype(BF16)
    h_next = _norm_mod(xn_ref[0], gain, mod[4:5], mod[3:4]) * (i < last).astype(F32)
    h_prev = _norm_mod(xp_ref[0], gain, mod[4:5], mod[3:4]) * (i > 0).astype(F32)
    h_ref[tile:] = jnp.concatenate([h_next, h_prev], axis=0).astype(BF16)
    h = h_ref[...]
    n_chunks = FF_DIM // FF_CHUNK
    sub = lax.broadcasted_iota(jnp.int32, (HALO, FF_CHUNK), 0)

    def up_proj(j):
        for half in range(2):
            c0 = half * FF_DIM + j * FF_CHUNK
            up = jnp.dot(h, wup_ref[:, c0:c0 + FF_CHUNK], preferred_element_type=F32)
            dst = up_ref.at[j % 2, half]
            dst[HALO:HALO + tile] = up[:tile]
            dst[0:HALO] = jnp.where(sub == 0, pltpu.roll(up[tile + HALO:], 1, 0),
                                    pltpu.roll(up[tile - HALO:tile], 1, 0))
            dst[HALO + tile:] = jnp.where(sub == HALO - 1, pltpu.roll(up[tile:tile + HALO], HALO - 1, 0),
                                          pltpu.roll(up[0:HALO], HALO - 1, 0))

    def conv(j, half):
        cols = slice(half * FF_DIM + j * FF_CHUNK, half * FF_DIM + (j + 1) * FF_CHUNK)
        w = wc_ref[:, cols]
        src = up_ref.at[j % 2, half]
        return (src[0:tile] * w[0:1] + src[HALO:HALO + tile] * w[1:2] + src[2 * HALO:2 * HALO + tile] * w[2:3]
                + bc_ref[:, cols])

    up_proj(0)
    for j in range(n_chunks):
        if j + 1 < n_chunks:
            up_proj(j + 1)
        a = conv(j, 0)
        g = conv(j, 1)
        act_ref[:, j * FF_CHUNK:(j + 1) * FF_CHUNK] = (a * (g * jax.nn.sigmoid(g))).astype(BF16)
    y = x + mod[5:6] * jnp.dot(act_ref[...], wdn_ref[...], preferred_element_type=F32)
    if final:
        y = y * lax.rsqrt(jnp.mean(y * y, axis=-1, keepdims=True) + EPS) * gf_ref[...]
    o_ref[0] = _regroup_rows(y, tile // HALO)


def _ffn_layer(x, mod, gain, w_up_all, w_conv, b_conv, w_down_all, g_final, layer, final):
    b, s, d = x.shape
    tile = min(2 * TOKEN_TILE, s)
    hpt = tile // HALO
    nh = s // HALO
    tok = lambda bi, i: (bi, i, 0)
    const2 = lambda bi, i: (0, 0)
    return pl.pallas_call(
        functools.partial(_ffn_kernel, tile=tile, final=final),
        grid=(b, s // tile),
        in_specs=[
            pl.BlockSpec((1, HALO, d), lambda bi, i: (bi, jnp.maximum(i * hpt - 1, 0), 0)),
            pl.BlockSpec((1, tile, d), tok),
            pl.BlockSpec((1, HALO, d), lambda bi, i: (bi, jnp.minimum((i + 1) * hpt, nh - 1), 0)),
            pl.BlockSpec((1, 6, d), lambda bi, i: (bi, 0, 0)),
            pl.BlockSpec((1, d), const2),
            pl.BlockSpec((None, d, 2 * FF_DIM), lambda bi, i: (layer, 0, 0)),
            pl.BlockSpec((3, 2 * FF_DIM), const2),
            pl.BlockSpec((1, 2 * FF_DIM), const2),
            pl.BlockSpec((None, FF_DIM, d), lambda bi, i: (layer, 0, 0)),
            pl.BlockSpec((1, d), const2),
        ],
        out_specs=pl.BlockSpec((1, tile, d), tok),
        out_shape=jax.ShapeDtypeStruct(x.shape, F32),
        scratch_shapes=[
            pltpu.VMEM((tile + 2 * HALO, d), BF16),
            pltpu.VMEM((2, 2, tile + 2 * HALO, FF_CHUNK), F32),
            pltpu.VMEM((tile, FF_DIM), BF16),
        ],
        compiler_params=_params("parallel", "parallel"),
        name="conv_ffn",
    )(x, x, x, mod, gain.reshape(1, d), w_up_all, w_conv, b_conv.reshape(1, 2 * FF_DIM), w_down_all,
      g_final.reshape(1, d))


def _trunk(x, mods, norm_g, a_w_in, a_g_v, a_w_s, a_b_s, a_w_out, b_w_qkv, b_sinks, b_w_o, c_w_in, c_w_out,
           f_w_up, f_w_conv, f_b_conv, f_w_down, g_final):
    for i in range(DEPTH):
        mod = mods[i]
        kind, j = i % N_MIXERS, i // N_MIXERS
        if kind == 0:
            x = _gmlp_layer(x, mod, norm_g[i, 0], a_w_in[j], a_g_v[j], a_w_s[j], a_b_s[j], a_w_out[j])
        elif kind == 1:
            q, k, v = _qkv_proj(x, mod, norm_g[i, 0], b_w_qkv[j])
            x = _attn_layer(x, mod, q, k, v, b_sinks[j], b_w_o[j])
        else:
            x = _fourier_layer(x, mod, norm_g[i, 0], c_w_in[j], c_w_out[j])
        x = _ffn_layer(x, mod, norm_g[i, 1], f_w_up, f_w_conv[i], f_b_conv[i], f_w_down, g_final, layer=i,
                       final=(i == DEPTH - 1))
    return x


def kernel(x_prompt, x_sample, c_prompt, c_sample, w_ada, b_ada, norm_g, a_w_in, a_g_v, a_w_s, a_b_s, a_w_out,
           b_w_qkv, b_sinks, b_w_o, c_w_in, c_w_out, f_w_up, f_w_conv, f_b_conv, f_w_down, g_final):
    nb_p, nb_s = x_prompt.shape[0], x_sample.shape[0]
    assert nb_p + nb_s <= MOD_ROWS
    c_all = jnp.concatenate(
        [c_prompt, c_sample, jnp.zeros((MOD_ROWS - nb_p - nb_s, D_MODEL), F32)], axis=0)
    mods = _modulation(c_all, w_ada, b_ada).reshape(DEPTH, MOD_ROWS, 6, D_MODEL)
    weights = (norm_g, a_w_in, a_g_v, a_w_s, a_b_s, a_w_out, b_w_qkv, b_sinks, b_w_o, c_w_in, c_w_out,
               f_w_up.astype(BF16), f_w_conv, f_b_conv, f_w_down.astype(BF16), g_final)
    y_prompt = _trunk(x_prompt, mods[:, :nb_p], *weights)
    y_sample = _trunk(x_sample, mods[:, nb_p:nb_p + nb_s], *weights)
    return (y_prompt, y_sample)
```

```python
import functools

import numpy as np
import jax
import jax.numpy as jnp
from jax import lax
from jax.experimental import pallas as pl
from jax.experimental.pallas import tpu as pltpu

F32 = jnp.float32
BF16 = jnp.bfloat16

D_MODEL = 1024
DEPTH = 4
N_MIXERS = 3
EPS = 1e-6
NEG_INF = -1e30
CHUNK = 128
A_GROUPS = 8
A_HEAD = D_MODEL // A_GROUPS
B_HEADS = 16
B_KV_HEADS = 4
B_HEAD_DIM = 64
B_Q_PER_KV = B_HEADS // B_KV_HEADS
WINDOW = 128
BLOCK = 128
Q_WIDTH = B_HEADS * B_HEAD_DIM
KV_WIDTH = B_KV_HEADS * B_HEAD_DIM
C_GROUPS = 8
C_GROUP_DIM = D_MODEL // C_GROUPS
FF_DIM = 2816
FF_CHUNK = 256
HALO = 8

MOD_ROWS = 8
MOD_NT = 1536
VMEM_LIMIT = 56 * 1024 * 1024

TOKEN_TILE = 512


def _params(*sem):
    return pltpu.CompilerParams(dimension_semantics=sem, vmem_limit_bytes=VMEM_LIMIT)


def _norm_mod(x, gain, scale, shift):
    ms = jnp.mean(x * x, axis=-1, keepdims=True)
    return x * lax.rsqrt(ms + EPS) * (gain * (1.0 + scale)) + shift


def _regroup_rows(x, outer):
    rows, cols = x.shape
    return jnp.transpose(x.reshape(outer, rows // outer, cols), (1, 0, 2)).reshape(rows, cols)


def _mod_kernel(c_ref, w_ref, b_ref, o_ref):
    c = c_ref[...]
    cs = c * jax.nn.sigmoid(c)
    o_ref[0] = jnp.dot(cs, w_ref[0], preferred_element_type=F32) + b_ref[0]


def _modulation(c_all, w_ada, b_ada):
    n_out = w_ada.shape[-1]
    return pl.pallas_call(
        _mod_kernel,
        grid=(DEPTH, n_out // MOD_NT),
        in_specs=[
            pl.BlockSpec((MOD_ROWS, D_MODEL), lambda i, j: (0, 0)),
            pl.BlockSpec((1, D_MODEL, MOD_NT), lambda i, j: (i, 0, j)),
            pl.BlockSpec((1, 1, MOD_NT), lambda i, j: (i, 0, j)),
        ],
        out_specs=pl.BlockSpec((1, MOD_ROWS, MOD_NT), lambda i, j: (i, 0, j)),
        out_shape=jax.ShapeDtypeStruct((DEPTH, MOD_ROWS, n_out), F32),
        compiler_params=_params("parallel", "parallel"),
        name="modulation",
    )(c_all, w_ada, b_ada.reshape(DEPTH, 1, n_out))


def _gmlp_kernel(x_ref, mod_ref, g_ref, win_ref, gv_ref, ws_ref, bs_ref, wout_ref, o_ref, gated_ref, *, tile):
    mod = mod_ref[0]
    half = tile // 2
    n_sub = half // CHUNK
    pair = 2 * A_HEAD
    n_pairs = D_MODEL // pair
    inv_sqrt2 = np.float32(1.0 / np.sqrt(2.0))

    def gelu(t):
        return 0.5 * t * (1.0 + lax.erf(t * inv_sqrt2))

    halves = [slice(i * half, (i + 1) * half) for i in range(2)]
    hs = [_norm_mod(x_ref[0, r, :], g_ref[...], mod[1:2], mod[0:1]).astype(BF16) for r in halves]
    vs = [jnp.dot(h, win_ref[:, D_MODEL:], preferred_element_type=F32) for h in hs]
    for i, r in enumerate(halves):
        h = hs[i]
        v = gelu(vs[i])
        v = v * lax.rsqrt(jnp.mean(v * v, axis=-1, keepdims=True) + EPS) * gv_ref[...]
        vb = v.astype(BF16)

        def u_proj(j):
            return jnp.dot(h, win_ref[:, j * pair:(j + 1) * pair], preferred_element_type=F32)

        u_next = u_proj(0)
        for j in range(n_pairs):
            u = gelu(u_next)
            if j + 1 < n_pairs:
                u_next = u_proj(j + 1)
            for gi in range(2):
                g = 2 * j + gi
                cols = slice(g * A_HEAD, (g + 1) * A_HEAD)
                rhs = jnp.concatenate([vb[c * CHUNK:(c + 1) * CHUNK, cols] for c in range(n_sub)], axis=1)
                sv = jnp.dot(ws_ref[g], rhs, preferred_element_type=F32)
                for c in range(n_sub):
                    rows = slice(c * CHUNK, (c + 1) * CHUNK)
                    gated_ref[i, rows, cols] = (u[rows, gi * A_HEAD:(gi + 1) * A_HEAD]
                                                * (sv[:, c * CHUNK:(c + 1) * CHUNK] + bs_ref[g])).astype(BF16)
        m = jnp.dot(gated_ref[i], wout_ref[...], preferred_element_type=F32)
        o_ref[0, r, :] = x_ref[0, r, :] + mod[2:3] * m


def _gmlp_layer(x, mod, gain, w_in, g_v, w_s, b_s, w_out):
    b, s, d = x.shape
    tile = min(2 * TOKEN_TILE, s)
    bs_full = jnp.broadcast_to(b_s[:, :, None], (A_GROUPS, CHUNK, A_HEAD))
    const2 = lambda bi, i: (0, 0)
    const3 = lambda bi, i: (0, 0, 0)
    return pl.pallas_call(
        functools.partial(_gmlp_kernel, tile=tile),
        grid=(b, s // tile),
        in_specs=[
            pl.BlockSpec((1, tile, d), lambda bi, i: (bi, i, 0)),
            pl.BlockSpec((1, 6, d), lambda bi, i: (bi, 0, 0)),
            pl.BlockSpec((1, d), const2),
            pl.BlockSpec((d, 2 * d), const2),
            pl.BlockSpec((1, d), const2),
            pl.BlockSpec((A_GROUPS, CHUNK, CHUNK), const3),
            pl.BlockSpec((A_GROUPS, CHUNK, A_HEAD), const3),
            pl.BlockSpec((d, d), const2),
        ],
        out_specs=pl.BlockSpec((1, tile, d), lambda bi, i: (bi, i, 0)),
        out_shape=jax.ShapeDtypeStruct(x.shape, F32),
        scratch_shapes=[pltpu.VMEM((2, tile // 2, d), BF16)],
        compiler_params=_params("parallel", "parallel"),
        name="gmlp_mixer",
    )(x, mod, gain.reshape(1, d), w_in.astype(BF16), g_v.reshape(1, d), w_s.astype(BF16), bs_full,
      w_out.astype(BF16))


_PAIRS = B_KV_HEADS // 2
_HEAD_ORDER = [((2 * (t // B_Q_PER_KV) + half) * B_Q_PER_KV + t % B_Q_PER_KV)
               for t in range(B_HEADS // 2) for half in range(2)]
_HEAD_COLS = np.concatenate([np.arange(h * B_HEAD_DIM, (h + 1) * B_HEAD_DIM) for h in _HEAD_ORDER])
GROUP_ROWS = B_Q_PER_KV * BLOCK
BAND = 3 * BLOCK
PAIR_WIDTH = 2 * B_HEAD_DIM
LOG2E = float(np.log2(np.e))
_NT = (((1,), (1,)), ((), ()))


def _qkv_kernel(x_ref, mod_ref, g_ref, wqk_ref, wvt_ref, q_ref, k_ref, vt_ref, *, tile):
    mod = mod_ref[0]
    h = _norm_mod(x_ref[0], g_ref[...], mod[1:2], mod[0:1]).astype(BF16)
    qk = jnp.dot(h, wqk_ref[...], preferred_element_type=F32)
    for blk in range(tile // BLOCK):
        rows = slice(blk * BLOCK, (blk + 1) * BLOCK)
        for t in range(Q_WIDTH // PAIR_WIDTH):
            q_ref[0, blk, t * BLOCK:(t + 1) * BLOCK, :] = (
                qk[rows, t * PAIR_WIDTH:(t + 1) * PAIR_WIDTH] * (B_HEAD_DIM ** -0.5 * LOG2E)).astype(BF16)
    k_ref[0] = qk[:, Q_WIDTH:].astype(BF16)
    vt_ref[0] = lax.dot_general(wvt_ref[...], h, _NT, preferred_element_type=F32).astype(BF16)


def _qkv_proj(x, mod, gain, w_qkv):
    b, s, d = x.shape
    tile = min(2 * TOKEN_TILE, s)
    tok = lambda bi, i: (bi, i, 0)
    const2 = lambda bi, i: (0, 0)
    w_qk = jnp.concatenate([w_qkv[:, _HEAD_COLS], w_qkv[:, Q_WIDTH:Q_WIDTH + KV_WIDTH]], axis=1).astype(BF16)
    w_vt = w_qkv[:, Q_WIDTH + KV_WIDTH:].T.astype(BF16)
    return pl.pallas_call(
        functools.partial(_qkv_kernel, tile=tile),
        grid=(b, s // tile),
        in_specs=[
            pl.BlockSpec((1, tile, d), tok),
            pl.BlockSpec((1, 6, d), lambda bi, i: (bi, 0, 0)),
            pl.BlockSpec((1, d), const2),
            pl.BlockSpec((d, Q_WIDTH + KV_WIDTH), const2),
            pl.BlockSpec((KV_WIDTH, d), const2),
        ],
        out_specs=[
            pl.BlockSpec((1, tile // BLOCK, Q_WIDTH, PAIR_WIDTH), lambda bi, i: (bi, i, 0, 0)),
            pl.BlockSpec((1, tile, KV_WIDTH), tok),
            pl.BlockSpec((1, KV_WIDTH, tile), lambda bi, i: (bi, 0, i)),
        ],
        out_shape=[
            jax.ShapeDtypeStruct((b, s // BLOCK, Q_WIDTH, PAIR_WIDTH), BF16),
            jax.ShapeDtypeStruct((b, s, KV_WIDTH), BF16),
            jax.ShapeDtypeStruct((b, KV_WIDTH, s), BF16),
        ],
        compiler_params=_params("parallel", "parallel"),
        name="qkv_proj",
    )(x, mod, gain.reshape(1, d), w_qk, w_vt)


def _attn_bias_table():
    qq = np.arange(BLOCK)[None, :]
    kk = np.arange(BAND)[:, None]
    dist = np.abs(qq + BLOCK - kk)
    slopes = np.exp2(np.float32(-8.0) * np.arange(1, B_HEADS + 1, dtype=np.float32) / np.float32(B_HEADS))
    tbl = np.empty((3, _PAIRS, 2 * BAND, GROUP_ROWS), np.float32)
    for case in range(3):
        valid = dist <= WINDOW
        if case == 0:
            valid = valid & (kk >= BLOCK)
        if case == 2:
            valid = valid & (kk < 2 * BLOCK)
        for p in range(_PAIRS):
            for c in range(B_Q_PER_KV):
                for half in range(2):
                    hd = (2 * p + half) * B_Q_PER_KV + c
                    alibi = (-slopes[hd].astype(np.float64) * dist * LOG2E).astype(np.float32)
                    tbl[case, p, half * BAND:(half + 1) * BAND, c * BLOCK:(c + 1) * BLOCK] = np.where(
                        valid, alibi, np.float32(NEG_INF))
    return tbl


def _attn_kernel(x_ref, mod_ref, q_ref, kp_ref, k_ref, kn_ref, vp_ref, v_ref, vn_ref, bias_ref, sink_ref, wo_ref,
                 o_ref, kext_ref, vext_ref, ocat_ref, *, tile, nblk):
    i = pl.program_id(1)
    bpt = tile // BLOCK
    kext_ref[0:BLOCK] = kp_ref[0]
    kext_ref[BLOCK:BLOCK + tile] = k_ref[0]
    kext_ref[BLOCK + tile:] = kn_ref[0]
    vext_ref[0] = vp_ref[0]
    for j in range(bpt):
        vext_ref[1 + j] = v_ref[0, :, j * BLOCK:(j + 1) * BLOCK]
    vext_ref[1 + bpt] = vn_ref[0]
    low_lane = lax.broadcasted_iota(jnp.int32, (BAND, PAIR_WIDTH), 1) < B_HEAD_DIM
    low_row = lax.broadcasted_iota(jnp.int32, (PAIR_WIDTH, BAND), 0) < B_HEAD_DIM
    low_out = lax.broadcasted_iota(jnp.int32, (PAIR_WIDTH, GROUP_ROWS), 0) < B_HEAD_DIM
    ones_row = lax.broadcasted_iota(jnp.int32, (2 * HALO, 2 * BAND), 0) < HALO
    first_half = lax.broadcasted_iota(jnp.int32, (2 * HALO, 2 * BAND), 1) < BAND
    ones = jnp.where(ones_row == first_half, 1.0, 0.0).astype(BF16)

    def scores(qb, p):
        rows = slice(p * PAIR_WIDTH, (p + 1) * PAIR_WIDTH)
        n = i * bpt + qb
        case = jnp.where(n == 0, 0, jnp.where(n == nblk - 1, 2, 1))
        kb = kext_ref[qb * BLOCK:qb * BLOCK + BAND, rows]
        zk = jnp.zeros_like(kb)
        kz = jnp.concatenate([jnp.where(low_lane, kb, zk), jnp.where(low_lane, zk, kb)], axis=0)
        q4 = q_ref[0, qb, p * GROUP_ROWS:(p + 1) * GROUP_ROWS, :]
        return lax.dot_general(kz, q4, _NT, preferred_element_type=F32) + bias_ref[case, p]

    def attend(qb, p, sc):
        rows = slice(p * PAIR_WIDTH, (p + 1) * PAIR_WIDTH)
        vt = jnp.concatenate([vext_ref[qb, rows, :], vext_ref[qb + 1, rows, :], vext_ref[qb + 2, rows, :]], axis=1)
        zv = jnp.zeros_like(vt)
        vz = jnp.concatenate([
            jnp.concatenate([jnp.where(low_row, vt, zv), jnp.where(low_row, zv, vt)], axis=1), ones], axis=0)
        probs = []
        shift = []
        for half in range(2):
            sh = sc[half * BAND:(half + 1) * BAND]
            sink = sink_ref[p, half]
            mx = jnp.maximum(jnp.max(sh, axis=0, keepdims=True), sink)
            probs.append(jnp.exp2(sh - mx).astype(BF16))
            shift.append(jnp.exp2(sink - mx))
        pv = jnp.dot(vz, jnp.concatenate(probs, axis=0), preferred_element_type=F32)
        inv_a = 1.0 / (pv[PAIR_WIDTH:PAIR_WIDTH + 1] + shift[0])
        inv_b = 1.0 / (pv[PAIR_WIDTH + HALO:PAIR_WIDTH + HALO + 1] + shift[1])
        on = pv[:PAIR_WIDTH] * jnp.where(low_out, inv_a, inv_b)
        for c in range(B_Q_PER_KV):
            t = p * B_Q_PER_KV + c
            ocat_ref[qb * BLOCK:(qb + 1) * BLOCK, t * PAIR_WIDTH:(t + 1) * PAIR_WIDTH] = on[:, c * BLOCK:(c + 1) * BLOCK].T

    units = [(qb, p) for qb in range(bpt) for p in range(_PAIRS)]
    sc = scores(*units[0])
    for u, unit in enumerate(units):
        sc_next = scores(*units[u + 1]) if u + 1 < len(units) else None
        attend(*unit, sc)
        sc = sc_next
    m = jnp.dot(ocat_ref[...].astype(BF16), wo_ref[...], preferred_element_type=F32)
    o_ref[0] = x_ref[0] + mod_ref[0][2:3] * m


def _attn_layer(x, mod, q, k, vt, sinks, w_o):
    b, s, d = x.shape
    tile = min(2 * TOKEN_TILE, s)
    bpt = tile // BLOCK
    nblk = s // BLOCK
    assert nblk >= 2
    tok = lambda bi, i: (bi, i, 0)
    prev = lambda bi, i: jnp.maximum(i * bpt - 1, 0)
    nxt = lambda bi, i: jnp.minimum((i + 1) * bpt, nblk - 1)
    bias = jnp.asarray(_attn_bias_table())
    head = np.array([[[(2 * p + half) * B_Q_PER_KV + c for c in range(B_Q_PER_KV)] for half in range(2)]
                     for p in range(_PAIRS)])
    sink_tbl = jnp.repeat(sinks.astype(F32)[head] * LOG2E, BLOCK, axis=-1)[:, :, None, :]
    return pl.pallas_call(
        functools.partial(_attn_kernel, tile=tile, nblk=nblk),
        grid=(b, s // tile),
        in_specs=[
            pl.BlockSpec((1, tile, d), tok),
            pl.BlockSpec((1, 6, d), lambda bi, i: (bi, 0, 0)),
            pl.BlockSpec((1, bpt, Q_WIDTH, PAIR_WIDTH), lambda bi, i: (bi, i, 0, 0)),
            pl.BlockSpec((1, BLOCK, KV_WIDTH), lambda bi, i: (bi, prev(bi, i), 0)),
            pl.BlockSpec((1, tile, KV_WIDTH), tok),
            pl.BlockSpec((1, BLOCK, KV_WIDTH), lambda bi, i: (bi, nxt(bi, i), 0)),
            pl.BlockSpec((1, KV_WIDTH, BLOCK), lambda bi, i: (bi, 0, prev(bi, i))),
            pl.BlockSpec((1, KV_WIDTH, tile), lambda bi, i: (bi, 0, i)),
            pl.BlockSpec((1, KV_WIDTH, BLOCK), lambda bi, i: (bi, 0, nxt(bi, i))),
            pl.BlockSpec(bias.shape, lambda bi, i: (0, 0, 0, 0)),
            pl.BlockSpec(sink_tbl.shape, lambda bi, i: (0, 0, 0, 0)),
            pl.BlockSpec((Q_WIDTH, d), lambda bi, i: (0, 0)),
        ],
        out_specs=pl.BlockSpec((1, tile, d), tok),
        out_shape=jax.ShapeDtypeStruct(x.shape, F32),
        scratch_shapes=[
            pltpu.VMEM((tile + 2 * BLOCK, KV_WIDTH), BF16),
            pltpu.VMEM((bpt + 2, KV_WIDTH, BLOCK), BF16),
            pltpu.VMEM((tile, Q_WIDTH), F32),
        ],
        compiler_params=_params("parallel", "parallel"),
        name="window_attention",
    )(x, mod, q, k, k, k, vt, vt, vt, bias, sink_tbl, w_o[_HEAD_COLS].astype(BF16))


def _dft_tables(seq):
    n1 = BLOCK
    n2 = seq // n1
    k1 = np.arange(n1, dtype=np.int64)
    s1 = np.arange(n1, dtype=np.int64)
    s2 = np.arange(n2, dtype=np.int64)
    pos = s1[None, None, :] * n2 + s2[:, None, None]
    ang = 2.0 * np.pi * ((k1[None, :, None] * pos) % seq).astype(np.float64) / seq
    stage1 = np.concatenate([np.cos(ang), -np.sin(ang)], axis=1).astype(np.float32)
    k2 = np.arange(n2, dtype=np.int64)
    ang2 = 2.0 * np.pi * ((k2[:, None] * s2[None, :]) % n2).astype(np.float64) / n2
    c2, sn2 = np.cos(ang2), np.sin(ang2)
    stage2 = np.block([[c2, sn2], [-sn2, c2]]).astype(np.float32)
    stage2 = stage2.reshape(2 * n2, 2, n2).transpose(0, 2, 1).reshape(2 * n2, 2 * n2)
    c = np.arange(C_GROUP_DIM, dtype=np.int64)
    angc = 2.0 * np.pi * ((c[:, None] * c[None, :]) % C_GROUP_DIM).astype(np.float64) / C_GROUP_DIM
    chan = (np.concatenate([np.cos(angc), np.sin(angc)], axis=0) / np.sqrt(float(seq) * C_GROUP_DIM)).astype(np.float32)
    return n1, n2, stage1, stage2, chan


def _fourier1_kernel(x_ref, mod_ref, g_ref, win_ref, dft_ref, mid_ref, *, n1, grp):
    mod = mod_ref[0]
    x = x_ref[0].reshape(n1 * grp, D_MODEL)
    h = _norm_mod(x, g_ref[...], mod[1:2], mod[0:1]).astype(BF16)
    z = jnp.dot(h, win_ref[...], preferred_element_type=F32)
    z = _regroup_rows(z, n1).astype(BF16)
    y = jnp.concatenate([jnp.dot(dft_ref[g], z[g * n1:(g + 1) * n1], preferred_element_type=F32)
                         for g in range(grp)], axis=0)
    mid_ref[0] = _regroup_rows(y, 2 * grp).astype(BF16).reshape(n1, 2 * grp, D_MODEL)


def _fourier2_kernel(x_ref, mod_ref, mid_ref, dft_ref, chan_ref, wout_ref, o_ref, re_ref, im_ref, f_ref, *, n2, grp):
    for kl in range(grp):
        spec = jnp.dot(dft_ref[...], mid_ref[0, kl], preferred_element_type=F32)
        re_ref[kl * n2:(kl + 1) * n2] = spec[:n2].astype(BF16)
        im_ref[kl * n2:(kl + 1) * n2] = spec[n2:].astype(BF16)
    for cg in range(C_GROUPS):
        cols = slice(cg * C_GROUP_DIM, (cg + 1) * C_GROUP_DIM)
        lhs = jnp.concatenate([re_ref[:, cols], im_ref[:, cols]], axis=1)
        f_ref[:, cols] = jnp.dot(lhs, chan_ref[...], preferred_element_type=F32).astype(BF16)
    m = jnp.dot(f_ref[...], wout_ref[...], preferred_element_type=F32)
    m = _regroup_rows(m, grp)
    out = x_ref[0].reshape(n2 * grp, D_MODEL) + mod_ref[0][2:3] * m
    o_ref[0] = out.reshape(n2, grp, D_MODEL)


def _fourier_layer(x, mod, gain, w_in, w_out):
    b, s, d = x.shape
    n1, n2, stage1, stage2, chan = _dft_tables(s)
    grp = HALO
    const2 = lambda bi, j: (0, 0)
    mid = pl.pallas_call(
        functools.partial(_fourier1_kernel, n1=n1, grp=grp),
        grid=(b, n2 // grp),
        in_specs=[
            pl.BlockSpec((1, n1, grp, d), lambda bi, j: (bi, 0, j, 0)),
            pl.BlockSpec((1, 6, d), lambda bi, j: (bi, 0, 0)),
            pl.BlockSpec((1, d), const2),
            pl.BlockSpec((d, d), const2),
            pl.BlockSpec((grp, 2 * n1, n1), lambda bi, j: (j, 0, 0)),
        ],
        out_specs=pl.BlockSpec((1, n1, 2 * grp, d), lambda bi, j: (bi, 0, j, 0)),
        out_shape=jax.ShapeDtypeStruct((b, n1, 2 * n2, d), BF16),
        compiler_params=_params("parallel", "parallel"),
        name="fourier_stage1",
    )(x.reshape(b, n1, n2, d), mod, gain.reshape(1, d), w_in.astype(BF16), jnp.asarray(stage1).astype(BF16))
    out = pl.pallas_call(
        functools.partial(_fourier2_kernel, n2=n2, grp=grp),
        grid=(b, n1 // grp),
        in_specs=[
            pl.BlockSpec((1, n2, grp, d), lambda bi, j: (bi, 0, j, 0)),
            pl.BlockSpec((1, 6, d), lambda bi, j: (bi, 0, 0)),
            pl.BlockSpec((1, grp, 2 * n2, d), lambda bi, j: (bi, j, 0, 0)),
            pl.BlockSpec((2 * n2, 2 * n2), const2),
            pl.BlockSpec((2 * C_GROUP_DIM, C_GROUP_DIM), const2),
            pl.BlockSpec((d, d), const2),
        ],
        out_specs=pl.BlockSpec((1, n2, grp, d), lambda bi, j: (bi, 0, j, 0)),
        out_shape=jax.ShapeDtypeStruct((b, n2, n1, d), F32),
        scratch_shapes=[pltpu.VMEM((grp * n2, d), BF16)] * 3,
        compiler_params=_params("parallel", "parallel"),
        name="fourier_stage2",
    )(x.reshape(b, n2, n1, d), mod, mid, jnp.asarray(stage2).astype(BF16),
      jnp.asarray(chan).astype(BF16), w_out.astype(BF16))
    return out.reshape(b, s, d)


def _ffn_kernel(xp_ref, x_ref, xn_ref, mod_ref, g_ref, wup_ref, wc_ref, bc_ref, wdn_ref, gf_ref, o_ref,
                h_ref, up_ref, act_ref, *, tile, final):
    i = pl.program_id(1)
    last = pl.num_programs(1) - 1
    mod = mod_ref[0]
    gain = g_ref[...]
    x = _regroup_rows(x_ref[0], HALO)
    h_ref[0:tile] = _norm_mod(x, gain, mod[4:5], mod[3:4]).astype(BF16)
    h_next = _norm_mod(xn_ref[0], gain, mod[4:5], mod[3:4]) * (i < last).astype(F32)
    h_prev = _norm_mod(xp_ref[0], gain, mod[4:5], mod[3:4]) * (i > 0).astype(F32)
    h_ref[tile:] = jnp.concatenate([h_next, h_prev], axis=0).astype(BF16)
    h = h_ref[...]
    n_chunks = FF_DIM // FF_CHUNK
    sub = lax.broadcasted_iota(jnp.int32, (HALO, FF_CHUNK), 0)

    def up_proj(j):
        for half in range(2):
            c0 = half * FF_DIM + j * FF_CHUNK
            up = jnp.dot(h, wup_ref[:, c0:c0 + FF_CHUNK], preferred_element_type=F32)
            dst = up_ref.at[j % 2, half]
            dst[HALO:HALO + tile] = up[:tile]
            dst[0:HALO] = jnp.where(sub == 0, pltpu.roll(up[tile + HALO:], 1, 0),
                                    pltpu.roll(up[tile - HALO:tile], 1, 0))
            dst[HALO + tile:] = jnp.where(sub == HALO - 1, pltpu.roll(up[tile:tile + HALO], HALO - 1, 0),
                                          pltpu.roll(up[0:HALO], HALO - 1, 0))

    def conv(j, half):
        cols = slice(half * FF_DIM + j * FF_CHUNK, half * FF_DIM + (j + 1) * FF_CHUNK)
        w = wc_ref[:, cols]
        src = up_ref.at[j % 2, half]
        return (src[0:tile] * w[0:1] + src[HALO:HALO + tile] * w[1:2] + src[2 * HALO:2 * HALO + tile] * w[2:3]
                + bc_ref[:, cols])

    up_proj(0)
    for j in range(n_chunks):
        if j + 1 < n_chunks:
            up_proj(j + 1)
        a = conv(j, 0)
        g = conv(j, 1)
        act_ref[:, j * FF_CHUNK:(j + 1) * FF_CHUNK] = (a * (g * jax.nn.sigmoid(g))).astype(BF16)
    y = x + mod[5:6] * jnp.dot(act_ref[...], wdn_ref[...], preferred_element_type=F32)
    if final:
        y = y * lax.rsqrt(jnp.mean(y * y, axis=-1, keepdims=True) + EPS) * gf_ref[...]
    o_ref[0] = _regroup_rows(y, tile // HALO)


def _ffn_layer(x, mod, gain, w_up_all, w_conv, b_conv, w_down_all, g_final, layer, final):
    b, s, d = x.shape
    tile = min(2 * TOKEN_TILE, s)
    hpt = tile // HALO
    nh = s // HALO
    tok = lambda bi, i: (bi, i, 0)
    const2 = lambda bi, i: (0, 0)
    return pl.pallas_call(
        functools.partial(_ffn_kernel, tile=tile, final=final),
        grid=(b, s // tile),
        in_specs=[
            pl.BlockSpec((1, HALO, d), lambda bi, i: (bi, jnp.maximum(i * hpt - 1, 0), 0)),
            pl.BlockSpec((1, tile, d), tok),
            pl.BlockSpec((1, HALO, d), lambda bi, i: (bi, jnp.minimum((i + 1) * hpt, nh - 1), 0)),
            pl.BlockSpec((1, 6, d), lambda bi, i: (bi, 0, 0)),
            pl.BlockSpec((1, d), const2),
            pl.BlockSpec((None, d, 2 * FF_DIM), lambda bi, i: (layer, 0, 0)),
            pl.BlockSpec((3, 2 * FF_DIM), const2),
            pl.BlockSpec((1, 2 * FF_DIM), const2),
            pl.BlockSpec((None, FF_DIM, d), lambda bi, i: (layer, 0, 0)),
            pl.BlockSpec((1, d), const2),
        ],
        out_specs=pl.BlockSpec((1, tile, d), tok),
        out_shape=jax.ShapeDtypeStruct(x.shape, F32),
        scratch_shapes=[
            pltpu.VMEM((tile + 2 * HALO, d), BF16),
            pltpu.VMEM((2, 2, tile + 2 * HALO, FF_CHUNK), F32),
            pltpu.VMEM((tile, FF_DIM), BF16),
        ],
        compiler_params=_params("parallel", "parallel"),
        name="conv_ffn",
    )(x, x, x, mod, gain.reshape(1, d), w_up_all, w_conv, b_conv.reshape(1, 2 * FF_DIM), w_down_all,
      g_final.reshape(1, d))


def _trunk(x, mods, norm_g, a_w_in, a_g_v, a_w_s, a_b_s, a_w_out, b_w_qkv, b_sinks, b_w_o, c_w_in, c_w_out,
           f_w_up, f_w_conv, f_b_conv, f_w_down, g_final):
    for i in range(DEPTH):
        mod = mods[i]
        kind, j = i % N_MIXERS, i // N_MIXERS
        if kind == 0:
            x = _gmlp_layer(x, mod, norm_g[i, 0], a_w_in[j], a_g_v[j], a_w_s[j], a_b_s[j], a_w_out[j])
        elif kind == 1:
            q, k, v = _qkv_proj(x, mod, norm_g[i, 0], b_w_qkv[j])
            x = _attn_layer(x, mod, q, k, v, b_sinks[j], b_w_o[j])
        else:
            x = _fourier_layer(x, mod, norm_g[i, 0], c_w_in[j], c_w_out[j])
        x = _ffn_layer(x, mod, norm_g[i, 1], f_w_up, f_w_conv[i], f_b_conv[i], f_w_down, g_final, layer=i,
                       final=(i == DEPTH - 1))
    return x


def kernel(x_prompt, x_sample, c_prompt, c_sample, w_ada, b_ada, norm_g, a_w_in, a_g_v, a_w_s, a_b_s, a_w_out,
           b_w_qkv, b_sinks, b_w_o, c_w_in, c_w_out, f_w_up, f_w_conv, f_b_conv, f_w_down, g_final):
    nb_p, nb_s = x_prompt.shape[0], x_sample.shape[0]
    assert nb_p + nb_s <= MOD_ROWS
    c_all = jnp.concatenate(
        [c_prompt, c_sample, jnp.zeros((MOD_ROWS - nb_p - nb_s, D_MODEL), F32)], axis=0)
    mods = _modulation(c_all, w_ada, b_ada).reshape(DEPTH, MOD_ROWS, 6, D_MODEL)
    weights = (norm_g, a_w_in, a_g_v, a_w_s, a_b_s, a_w_out, b_w_qkv, b_sinks, b_w_o, c_w_in, c_w_out,
               f_w_up.astype(BF16), f_w_conv, f_b_conv, f_w_down.astype(BF16), g_final)
    y_prompt = _trunk(x_prompt, mods[:, :nb_p], *weights)
    y_sample = _trunk(x_sample, mods[:, nb_p:nb_p + nb_s], *weights)
    return (y_prompt, y_sample)
```

```python
import functools

import numpy as np
import jax
import jax.numpy as jnp
from jax import lax
from jax.experimental import pallas as pl
from jax.experimental.pallas import tpu as pltpu

F32 = jnp.float32
BF16 = jnp.bfloat16

D_MODEL = 1024
DEPTH = 4
N_MIXERS = 3
EPS = 1e-6
NEG_INF = -1e30
CHUNK = 128
A_GROUPS = 8
A_HEAD = D_MODEL // A_GROUPS
B_HEADS = 16
B_KV_HEADS = 4
B_HEAD_DIM = 64
B_Q_PER_KV = B_HEADS // B_KV_HEADS
WINDOW = 128
BLOCK = 128
Q_WIDTH = B_HEADS * B_HEAD_DIM
KV_WIDTH = B_KV_HEADS * B_HEAD_DIM
C_GROUPS = 8
C_GROUP_DIM = D_MODEL // C_GROUPS
FF_DIM = 2816
FF_CHUNK = 256
HALO = 8
X_SLOTS = 3

MOD_ROWS = 8
MOD_NT = 1536
VMEM_LIMIT = 56 * 1024 * 1024

TOKEN_TILE = 512


def _params(*sem):
    return pltpu.CompilerParams(dimension_semantics=sem, vmem_limit_bytes=VMEM_LIMIT)


def _norm_mod(x, gain, scale, shift):
    ms = jnp.mean(x * x, axis=-1, keepdims=True)
    return x * lax.rsqrt(ms + EPS) * (gain * (1.0 + scale)) + shift


def _regroup_rows(x, outer):
    rows, cols = x.shape
    return jnp.transpose(x.reshape(outer, rows // outer, cols), (1, 0, 2)).reshape(rows, cols)


def _mod_kernel(c_ref, w_ref, b_ref, o_ref):
    c = c_ref[...]
    cs = c * jax.nn.sigmoid(c)
    o_ref[0] = jnp.dot(cs, w_ref[0], preferred_element_type=F32) + b_ref[0]


def _modulation(c_all, w_ada, b_ada):
    n_out = w_ada.shape[-1]
    return pl.pallas_call(
        _mod_kernel,
        grid=(DEPTH, n_out // MOD_NT),
        in_specs=[
            pl.BlockSpec((MOD_ROWS, D_MODEL), lambda i, j: (0, 0)),
            pl.BlockSpec((1, D_MODEL, MOD_NT), lambda i, j: (i, 0, j)),
            pl.BlockSpec((1, 1, MOD_NT), lambda i, j: (i, 0, j)),
        ],
        out_specs=pl.BlockSpec((1, MOD_ROWS, MOD_NT), lambda i, j: (i, 0, j)),
        out_shape=jax.ShapeDtypeStruct((DEPTH, MOD_ROWS, n_out), F32),
        compiler_params=_params("parallel", "parallel"),
        name="modulation",
    )(c_all, w_ada, b_ada.reshape(DEPTH, 1, n_out))


def _gmlp_kernel(x_ref, mod_ref, g_ref, win_ref, gv_ref, ws_ref, bs_ref, wout_ref, o_ref, gated_ref, *, tile):
    mod = mod_ref[0]
    half = tile // 2
    n_sub = half // CHUNK
    pair = 2 * A_HEAD
    n_pairs = D_MODEL // pair
    inv_sqrt2 = np.float32(1.0 / np.sqrt(2.0))

    def gelu(t):
        return 0.5 * t * (1.0 + lax.erf(t * inv_sqrt2))

    halves = [slice(i * half, (i + 1) * half) for i in range(2)]
    hs = [_norm_mod(x_ref[0, r, :], g_ref[...], mod[1:2], mod[0:1]).astype(BF16) for r in halves]
    vs = [jnp.dot(h, win_ref[:, D_MODEL:], preferred_element_type=F32) for h in hs]
    for i, r in enumerate(halves):
        h = hs[i]
        v = gelu(vs[i])
        v = v * lax.rsqrt(jnp.mean(v * v, axis=-1, keepdims=True) + EPS) * gv_ref[...]
        vb = v.astype(BF16)

        def u_proj(j):
            return jnp.dot(h, win_ref[:, j * pair:(j + 1) * pair], preferred_element_type=F32)

        u_next = u_proj(0)
        for j in range(n_pairs):
            u = gelu(u_next)
            if j + 1 < n_pairs:
                u_next = u_proj(j + 1)
            for gi in range(2):
                g = 2 * j + gi
                cols = slice(g * A_HEAD, (g + 1) * A_HEAD)
                rhs = jnp.concatenate([vb[c * CHUNK:(c + 1) * CHUNK, cols] for c in range(n_sub)], axis=1)
                sv = jnp.dot(ws_ref[g], rhs, preferred_element_type=F32)
                for c in range(n_sub):
                    rows = slice(c * CHUNK, (c + 1) * CHUNK)
                    gated_ref[i, rows, cols] = (u[rows, gi * A_HEAD:(gi + 1) * A_HEAD]
                                                * (sv[:, c * CHUNK:(c + 1) * CHUNK] + bs_ref[g])).astype(BF16)
        m = jnp.dot(gated_ref[i], wout_ref[...], preferred_element_type=F32)
        o_ref[0, r, :] = x_ref[0, r, :] + mod[2:3] * m


def _gmlp_layer(x, mod, gain, w_in, g_v, w_s, b_s, w_out):
    b, s, d = x.shape
    tile = min(2 * TOKEN_TILE, s)
    bs_full = jnp.broadcast_to(b_s[:, :, None], (A_GROUPS, CHUNK, A_HEAD))
    const2 = lambda bi, i: (0, 0)
    const3 = lambda bi, i: (0, 0, 0)
    return pl.pallas_call(
        functools.partial(_gmlp_kernel, tile=tile),
        grid=(b, s // tile),
        in_specs=[
            pl.BlockSpec((1, tile, d), lambda bi, i: (bi, i, 0)),
            pl.BlockSpec((1, 6, d), lambda bi, i: (bi, 0, 0)),
            pl.BlockSpec((1, d), const2),
            pl.BlockSpec((d, 2 * d), const2),
            pl.BlockSpec((1, d), const2),
            pl.BlockSpec((A_GROUPS, CHUNK, CHUNK), const3),
            pl.BlockSpec((A_GROUPS, CHUNK, A_HEAD), const3),
            pl.BlockSpec((d, d), const2),
        ],
        out_specs=pl.BlockSpec((1, tile, d), lambda bi, i: (bi, i, 0)),
        out_shape=jax.ShapeDtypeStruct(x.shape, F32),
        scratch_shapes=[pltpu.VMEM((2, tile // 2, d), BF16)],
        compiler_params=_params("parallel", "parallel"),
        name="gmlp_mixer",
    )(x, mod, gain.reshape(1, d), w_in.astype(BF16), g_v.reshape(1, d), w_s.astype(BF16), bs_full,
      w_out.astype(BF16))


_PAIRS = B_KV_HEADS // 2
_HEAD_ORDER = [((2 * (t // B_Q_PER_KV) + half) * B_Q_PER_KV + t % B_Q_PER_KV)
               for t in range(B_HEADS // 2) for half in range(2)]
_HEAD_COLS = np.concatenate([np.arange(h * B_HEAD_DIM, (h + 1) * B_HEAD_DIM) for h in _HEAD_ORDER])
GROUP_ROWS = B_Q_PER_KV * BLOCK
BAND = 3 * BLOCK
PAIR_WIDTH = 2 * B_HEAD_DIM
LOG2E = float(np.log2(np.e))
_NT = (((1,), (1,)), ((), ()))


def _qkv_kernel(x_ref, mod_ref, g_ref, wqk_ref, wvt_ref, q_ref, k_ref, vt_ref, *, tile):
    mod = mod_ref[0]
    h = _norm_mod(x_ref[0], g_ref[...], mod[1:2], mod[0:1]).astype(BF16)
    qk = jnp.dot(h, wqk_ref[...], preferred_element_type=F32)
    for blk in range(tile // BLOCK):
        rows = slice(blk * BLOCK, (blk + 1) * BLOCK)
        for t in range(Q_WIDTH // PAIR_WIDTH):
            q_ref[0, blk, t * BLOCK:(t + 1) * BLOCK, :] = (
                qk[rows, t * PAIR_WIDTH:(t + 1) * PAIR_WIDTH] * (B_HEAD_DIM ** -0.5 * LOG2E)).astype(BF16)
    k_ref[0] = qk[:, Q_WIDTH:].astype(BF16)
    vt_ref[0] = lax.dot_general(wvt_ref[...], h, _NT, preferred_element_type=F32).astype(BF16)


def _qkv_proj(x, mod, gain, w_qkv):
    b, s, d = x.shape
    tile = min(2 * TOKEN_TILE, s)
    tok = lambda bi, i: (bi, i, 0)
    const2 = lambda bi, i: (0, 0)
    w_qk = jnp.concatenate([w_qkv[:, _HEAD_COLS], w_qkv[:, Q_WIDTH:Q_WIDTH + KV_WIDTH]], axis=1).astype(BF16)
    w_vt = w_qkv[:, Q_WIDTH + KV_WIDTH:].T.astype(BF16)
    return pl.pallas_call(
        functools.partial(_qkv_kernel, tile=tile),
        grid=(b, s // tile),
        in_specs=[
            pl.BlockSpec((1, tile, d), tok),
            pl.BlockSpec((1, 6, d), lambda bi, i: (bi, 0, 0)),
            pl.BlockSpec((1, d), const2),
            pl.BlockSpec((d, Q_WIDTH + KV_WIDTH), const2),
            pl.BlockSpec((KV_WIDTH, d), const2),
        ],
        out_specs=[
            pl.BlockSpec((1, tile // BLOCK, Q_WIDTH, PAIR_WIDTH), lambda bi, i: (bi, i, 0, 0)),
            pl.BlockSpec((1, tile, KV_WIDTH), tok),
            pl.BlockSpec((1, KV_WIDTH, tile), lambda bi, i: (bi, 0, i)),
        ],
        out_shape=[
            jax.ShapeDtypeStruct((b, s // BLOCK, Q_WIDTH, PAIR_WIDTH), BF16),
            jax.ShapeDtypeStruct((b, s, KV_WIDTH), BF16),
            jax.ShapeDtypeStruct((b, KV_WIDTH, s), BF16),
        ],
        compiler_params=_params("parallel", "parallel"),
        name="qkv_proj",
    )(x, mod, gain.reshape(1, d), w_qk, w_vt)


def _attn_bias_table():
    qq = np.arange(BLOCK)[None, :]
    kk = np.arange(BAND)[:, None]
    dist = np.abs(qq + BLOCK - kk)
    slopes = np.exp2(np.float32(-8.0) * np.arange(1, B_HEADS + 1, dtype=np.float32) / np.float32(B_HEADS))
    tbl = np.empty((3, _PAIRS, 2 * BAND, GROUP_ROWS), np.float32)
    for case in range(3):
        valid = dist <= WINDOW
        if case == 0:
            valid = valid & (kk >= BLOCK)
        if case == 2:
            valid = valid & (kk < 2 * BLOCK)
        for p in range(_PAIRS):
            for c in range(B_Q_PER_KV):
                for half in range(2):
                    hd = (2 * p + half) * B_Q_PER_KV + c
                    alibi = (-slopes[hd].astype(np.float64) * dist * LOG2E).astype(np.float32)
                    tbl[case, p, half * BAND:(half + 1) * BAND, c * BLOCK:(c + 1) * BLOCK] = np.where(
                        valid, alibi, np.float32(NEG_INF))
    return tbl


def _attn_kernel(x_ref, mod_ref, q_ref, kp_ref, k_ref, kn_ref, vp_ref, v_ref, vn_ref, bias_ref, sink_ref, wo_ref,
                 o_ref, kext_ref, vext_ref, ocat_ref, *, tile, nblk):
    i = pl.program_id(1)
    bpt = tile // BLOCK
    kext_ref[0:BLOCK] = kp_ref[0]
    kext_ref[BLOCK:BLOCK + tile] = k_ref[0]
    kext_ref[BLOCK + tile:] = kn_ref[0]
    vext_ref[0] = vp_ref[0]
    for j in range(bpt):
        vext_ref[1 + j] = v_ref[0, :, j * BLOCK:(j + 1) * BLOCK]
    vext_ref[1 + bpt] = vn_ref[0]
    low_lane = lax.broadcasted_iota(jnp.int32, (BAND, PAIR_WIDTH), 1) < B_HEAD_DIM
    low_row = lax.broadcasted_iota(jnp.int32, (PAIR_WIDTH, BAND), 0) < B_HEAD_DIM
    low_out = lax.broadcasted_iota(jnp.int32, (PAIR_WIDTH, GROUP_ROWS), 0) < B_HEAD_DIM
    ones_row = lax.broadcasted_iota(jnp.int32, (2 * HALO, 2 * BAND), 0) < HALO
    first_half = lax.broadcasted_iota(jnp.int32, (2 * HALO, 2 * BAND), 1) < BAND
    ones = jnp.where(ones_row == first_half, 1.0, 0.0).astype(BF16)

    def scores(qb, p):
        rows = slice(p * PAIR_WIDTH, (p + 1) * PAIR_WIDTH)
        n = i * bpt + qb
        case = jnp.where(n == 0, 0, jnp.where(n == nblk - 1, 2, 1))
        kb = kext_ref[qb * BLOCK:qb * BLOCK + BAND, rows]
        zk = jnp.zeros_like(kb)
        kz = jnp.concatenate([jnp.where(low_lane, kb, zk), jnp.where(low_lane, zk, kb)], axis=0)
        q4 = q_ref[0, qb, p * GROUP_ROWS:(p + 1) * GROUP_ROWS, :]
        return lax.dot_general(kz, q4, _NT, preferred_element_type=F32) + bias_ref[case, p]

    def attend(qb, p, sc):
        rows = slice(p * PAIR_WIDTH, (p + 1) * PAIR_WIDTH)
        vt = jnp.concatenate([vext_ref[qb, rows, :], vext_ref[qb + 1, rows, :], vext_ref[qb + 2, rows, :]], axis=1)
        zv = jnp.zeros_like(vt)
        vz = jnp.concatenate([
            jnp.concatenate([jnp.where(low_row, vt, zv), jnp.where(low_row, zv, vt)], axis=1), ones], axis=0)
        probs = []
        shift = []
        for half in range(2):
            sh = sc[half * BAND:(half + 1) * BAND]
            sink = sink_ref[p, half]
            mx = jnp.maximum(jnp.max(sh, axis=0, keepdims=True), sink)
            probs.append(jnp.exp2(sh - mx).astype(BF16))
            shift.append(jnp.exp2(sink - mx))
        pv = jnp.dot(vz, jnp.concatenate(probs, axis=0), preferred_element_type=F32)
        inv_a = 1.0 / (pv[PAIR_WIDTH:PAIR_WIDTH + 1] + shift[0])
        inv_b = 1.0 / (pv[PAIR_WIDTH + HALO:PAIR_WIDTH + HALO + 1] + shift[1])
        on = pv[:PAIR_WIDTH] * jnp.where(low_out, inv_a, inv_b)
        for c in range(B_Q_PER_KV):
            t = p * B_Q_PER_KV + c
            ocat_ref[qb * BLOCK:(qb + 1) * BLOCK, t * PAIR_WIDTH:(t + 1) * PAIR_WIDTH] = on[:, c * BLOCK:(c + 1) * BLOCK].T

    units = [(qb, p) for qb in range(bpt) for p in range(_PAIRS)]
    sc = scores(*units[0])
    for u, unit in enumerate(units):
        sc_next = scores(*units[u + 1]) if u + 1 < len(units) else None
        attend(*unit, sc)
        sc = sc_next
    m = jnp.dot(ocat_ref[...].astype(BF16), wo_ref[...], preferred_element_type=F32)
    o_ref[0] = x_ref[0] + mod_ref[0][2:3] * m


def _attn_layer(x, mod, q, k, vt, sinks, w_o):
    b, s, d = x.shape
    tile = min(2 * TOKEN_TILE, s)
    bpt = tile // BLOCK
    nblk = s // BLOCK
    assert nblk >= 2
    tok = lambda bi, i: (bi, i, 0)
    prev = lambda bi, i: jnp.maximum(i * bpt - 1, 0)
    nxt = lambda bi, i: jnp.minimum((i + 1) * bpt, nblk - 1)
    bias = jnp.asarray(_attn_bias_table())
    head = np.array([[[(2 * p + half) * B_Q_PER_KV + c for c in range(B_Q_PER_KV)] for half in range(2)]
                     for p in range(_PAIRS)])
    sink_tbl = jnp.repeat(sinks.astype(F32)[head] * LOG2E, BLOCK, axis=-1)[:, :, None, :]
    return pl.pallas_call(
        functools.partial(_attn_kernel, tile=tile, nblk=nblk),
        grid=(b, s // tile),
        in_specs=[
            pl.BlockSpec((1, tile, d), tok),
            pl.BlockSpec((1, 6, d), lambda bi, i: (bi, 0, 0)),
            pl.BlockSpec((1, bpt, Q_WIDTH, PAIR_WIDTH), lambda bi, i: (bi, i, 0, 0)),
            pl.BlockSpec((1, BLOCK, KV_WIDTH), lambda bi, i: (bi, prev(bi, i), 0)),
            pl.BlockSpec((1, tile, KV_WIDTH), tok),
            pl.BlockSpec((1, BLOCK, KV_WIDTH), lambda bi, i: (bi, nxt(bi, i), 0)),
            pl.BlockSpec((1, KV_WIDTH, BLOCK), lambda bi, i: (bi, 0, prev(bi, i))),
            pl.BlockSpec((1, KV_WIDTH, tile), lambda bi, i: (bi, 0, i)),
            pl.BlockSpec((1, KV_WIDTH, BLOCK), lambda bi, i: (bi, 0, nxt(bi, i))),
            pl.BlockSpec(bias.shape, lambda bi, i: (0, 0, 0, 0)),
            pl.BlockSpec(sink_tbl.shape, lambda bi, i: (0, 0, 0, 0)),
            pl.BlockSpec((Q_WIDTH, d), lambda bi, i: (0, 0)),
        ],
        out_specs=pl.BlockSpec((1, tile, d), tok),
        out_shape=jax.ShapeDtypeStruct(x.shape, F32),
        scratch_shapes=[
            pltpu.VMEM((tile + 2 * BLOCK, KV_WIDTH), BF16),
            pltpu.VMEM((bpt + 2, KV_WIDTH, BLOCK), BF16),
            pltpu.VMEM((tile, Q_WIDTH), F32),
        ],
        compiler_params=_params("parallel", "parallel"),
        name="window_attention",
    )(x, mod, q, k, k, k, vt, vt, vt, bias, sink_tbl, w_o[_HEAD_COLS].astype(BF16))


def _dft_tables(seq):
    n1 = BLOCK
    n2 = seq // n1
    k1 = np.arange(n1, dtype=np.int64)
    s1 = np.arange(n1, dtype=np.int64)
    s2 = np.arange(n2, dtype=np.int64)
    pos = s1[None, None, :] * n2 + s2[:, None, None]
    ang = 2.0 * np.pi * ((k1[None, :, None] * pos) % seq).astype(np.float64) / seq
    stage1 = np.concatenate([np.cos(ang), -np.sin(ang)], axis=1).astype(np.float32)
    k2 = np.arange(n2, dtype=np.int64)
    ang2 = 2.0 * np.pi * ((k2[:, None] * s2[None, :]) % n2).astype(np.float64) / n2
    c2, sn2 = np.cos(ang2), np.sin(ang2)
    stage2 = np.block([[c2, sn2], [-sn2, c2]]).astype(np.float32)
    stage2 = stage2.reshape(2 * n2, 2, n2).transpose(0, 2, 1).reshape(2 * n2, 2 * n2)
    c = np.arange(C_GROUP_DIM, dtype=np.int64)
    angc = 2.0 * np.pi * ((c[:, None] * c[None, :]) % C_GROUP_DIM).astype(np.float64) / C_GROUP_DIM
    chan = (np.concatenate([np.cos(angc), np.sin(angc)], axis=0) / np.sqrt(float(seq) * C_GROUP_DIM)).astype(np.float32)
    return n1, n2, stage1, stage2, chan


def _fourier1_kernel(x_ref, mod_ref, g_ref, win_ref, dft_ref, mid_ref, *, n1, grp):
    mod = mod_ref[0]
    x = x_ref[0].reshape(n1 * grp, D_MODEL)
    h = _norm_mod(x, g_ref[...], mod[1:2], mod[0:1]).astype(BF16)
    z = jnp.dot(h, win_ref[...], preferred_element_type=F32)
    z = _regroup_rows(z, n1).astype(BF16)
    y = jnp.concatenate([jnp.dot(dft_ref[g], z[g * n1:(g + 1) * n1], preferred_element_type=F32)
                         for g in range(grp)], axis=0)
    mid_ref[0] = _regroup_rows(y, 2 * grp).astype(BF16).reshape(n1, 2 * grp, D_MODEL)


def _fourier2_kernel(x_ref, mod_ref, mid_ref, dft_ref, chan_ref, wout_ref, o_ref, re_ref, im_ref, f_ref, *, n2, grp):
    for kl in range(grp):
        spec = jnp.dot(dft_ref[...], mid_ref[0, kl], preferred_element_type=F32)
        re_ref[kl * n2:(kl + 1) * n2] = spec[:n2].astype(BF16)
        im_ref[kl * n2:(kl + 1) * n2] = spec[n2:].astype(BF16)
    for cg in range(C_GROUPS):
        cols = slice(cg * C_GROUP_DIM, (cg + 1) * C_GROUP_DIM)
        lhs = jnp.concatenate([re_ref[:, cols], im_ref[:, cols]], axis=1)
        f_ref[:, cols] = jnp.dot(lhs, chan_ref[...], preferred_element_type=F32).astype(BF16)
    m = jnp.dot(f_ref[...], wout_ref[...], preferred_element_type=F32)
    m = _regroup_rows(m, grp)
    out = x_ref[0].reshape(n2 * grp, D_MODEL) + mod_ref[0][2:3] * m
    o_ref[0] = out.reshape(n2, grp, D_MODEL)


def _fourier_layer(x, mod, gain, w_in, w_out):
    b, s, d = x.shape
    n1, n2, stage1, stage2, chan = _dft_tables(s)
    grp = HALO
    const2 = lambda bi, j: (0, 0)
    mid = pl.pallas_call(
        functools.partial(_fourier1_kernel, n1=n1, grp=grp),
        grid=(b, n2 // grp),
        in_specs=[
            pl.BlockSpec((1, n1, grp, d), lambda bi, j: (bi, 0, j, 0)),
            pl.BlockSpec((1, 6, d), lambda bi, j: (bi, 0, 0)),
            pl.BlockSpec((1, d), const2),
            pl.BlockSpec((d, d), const2),
            pl.BlockSpec((grp, 2 * n1, n1), lambda bi, j: (j, 0, 0)),
        ],
        out_specs=pl.BlockSpec((1, n1, 2 * grp, d), lambda bi, j: (bi, 0, j, 0)),
        out_shape=jax.ShapeDtypeStruct((b, n1, 2 * n2, d), BF16),
        compiler_params=_params("parallel", "parallel"),
        name="fourier_stage1",
    )(x.reshape(b, n1, n2, d), mod, gain.reshape(1, d), w_in.astype(BF16), jnp.asarray(stage1).astype(BF16))
    out = pl.pallas_call(
        functools.partial(_fourier2_kernel, n2=n2, grp=grp),
        grid=(b, n1 // grp),
        in_specs=[
            pl.BlockSpec((1, n2, grp, d), lambda bi, j: (bi, 0, j, 0)),
            pl.BlockSpec((1, 6, d), lambda bi, j: (bi, 0, 0)),
            pl.BlockSpec((1, grp, 2 * n2, d), lambda bi, j: (bi, j, 0, 0)),
            pl.BlockSpec((2 * n2, 2 * n2), const2),
            pl.BlockSpec((2 * C_GROUP_DIM, C_GROUP_DIM), const2),
            pl.BlockSpec((d, d), const2),
        ],
        out_specs=pl.BlockSpec((1, n2, grp, d), lambda bi, j: (bi, 0, j, 0)),
        out_shape=jax.ShapeDtypeStruct((b, n2, n1, d), F32),
        scratch_shapes=[pltpu.VMEM((grp * n2, d), BF16)] * 3,
        compiler_params=_params("parallel", "parallel"),
        name="fourier_stage2",
    )(x.reshape(b, n2, n1, d), mod, mid, jnp.asarray(stage2).astype(BF16),
      jnp.asarray(chan).astype(BF16), w_out.astype(BF16))
    return out.reshape(b, s, d)


def _ffn_kernel(xp_ref, x_hbm, xn_ref, mod_ref, g_ref, wup_ref, wc_ref, bc_ref, wdn_ref, gf_ref, o_ref,
                xbuf_ref, sem_ref, h_ref, up_ref, act_ref, *, tile, tps, n_tiles, final):
    t = pl.program_id(0)

    def tile_copy(k):
        rows = pl.ds(pl.multiple_of(k * tile, tile), tile)
        return pltpu.make_async_copy(x_hbm.at[rows], xbuf_ref.at[k % X_SLOTS], sem_ref.at[k % X_SLOTS])

    @pl.when(t == 0)
    def _():
        for k in range(min(X_SLOTS - 1, n_tiles)):
            tile_copy(k).start()

    @pl.when(t + (X_SLOTS - 1) < n_tiles)
    def _():
        tile_copy(t + (X_SLOTS - 1)).start()

    tile_copy(t).wait()
    i = t % tps
    last = tps - 1
    mod = mod_ref[0]
    gain = g_ref[...]
    x = _regroup_rows(xbuf_ref[t % X_SLOTS], HALO)
    h_ref[0:tile] = _norm_mod(x, gain, mod[4:5], mod[3:4]).astype(BF16)
    h_next = _norm_mod(xn_ref[...], gain, mod[4:5], mod[3:4]) * (i < last).astype(F32)
    h_prev = _norm_mod(xp_ref[...], gain, mod[4:5], mod[3:4]) * (i > 0).astype(F32)
    h_ref[tile:] = jnp.concatenate([h_next, h_prev], axis=0).astype(BF16)
    h = h_ref[...]
    n_chunks = FF_DIM // FF_CHUNK
    sub = lax.broadcasted_iota(jnp.int32, (HALO, FF_CHUNK), 0)

    def up_proj(j):
        for half in range(2):
            c0 = half * FF_DIM + j * FF_CHUNK
            up = jnp.dot(h, wup_ref[:, c0:c0 + FF_CHUNK], preferred_element_type=F32)
            dst = up_ref.at[j % 2, half]
            dst[HALO:HALO + tile] = up[:tile]
            dst[0:HALO] = jnp.where(sub == 0, pltpu.roll(up[tile + HALO:], 1, 0),
                                    pltpu.roll(up[tile - HALO:tile], 1, 0))
            dst[HALO + tile:] = jnp.where(sub == HALO - 1, pltpu.roll(up[tile:tile + HALO], HALO - 1, 0),
                                          pltpu.roll(up[0:HALO], HALO - 1, 0))

    def conv(j, half):
        cols = slice(half * FF_DIM + j * FF_CHUNK, half * FF_DIM + (j + 1) * FF_CHUNK)
        w = wc_ref[:, cols]
        src = up_ref.at[j % 2, half]
        return (src[0:tile] * w[0:1] + src[HALO:HALO + tile] * w[1:2] + src[2 * HALO:2 * HALO + tile] * w[2:3]
                + bc_ref[:, cols])

    up_proj(0)
    for j in range(n_chunks):
        if j + 1 < n_chunks:
            up_proj(j + 1)
        a = conv(j, 0)
        g = conv(j, 1)
        act_ref[:, j * FF_CHUNK:(j + 1) * FF_CHUNK] = (a * (g * jax.nn.sigmoid(g))).astype(BF16)
    y = x + mod[5:6] * jnp.dot(act_ref[...], wdn_ref[...], preferred_element_type=F32)
    if final:
        y = y * lax.rsqrt(jnp.mean(y * y, axis=-1, keepdims=True) + EPS) * gf_ref[...]
    o_ref[...] = _regroup_rows(y, tile // HALO)


def _ffn_layer(x, mod, gain, w_up_all, w_conv, b_conv, w_down_all, g_final, layer, final):
    b, s, d = x.shape
    tile = min(2 * TOKEN_TILE, s)
    tps = s // tile
    n_tiles = b * tps
    hpt = tile // HALO
    n_halo = b * s // HALO
    const2 = lambda t: (0, 0)
    x2 = x.reshape(b * s, d)
    out = pl.pallas_call(
        functools.partial(_ffn_kernel, tile=tile, tps=tps, n_tiles=n_tiles, final=final),
        grid=(n_tiles,),
        in_specs=[
            pl.BlockSpec((HALO, d), lambda t: (jnp.maximum(t * hpt - 1, 0), 0)),
            pl.BlockSpec(memory_space=pl.ANY),
            pl.BlockSpec((HALO, d), lambda t: (jnp.minimum((t + 1) * hpt, n_halo - 1), 0)),
            pl.BlockSpec((1, 6, d), lambda t: (t // tps, 0, 0)),
            pl.BlockSpec((1, d), const2),
            pl.BlockSpec((None, d, 2 * FF_DIM), lambda t: (layer, 0, 0)),
            pl.BlockSpec((3, 2 * FF_DIM), const2),
            pl.BlockSpec((1, 2 * FF_DIM), const2),
            pl.BlockSpec((None, FF_DIM, d), lambda t: (layer, 0, 0)),
            pl.BlockSpec((1, d), const2),
        ],
        out_specs=pl.BlockSpec((tile, d), lambda t: (t, 0)),
        out_shape=jax.ShapeDtypeStruct((b * s, d), F32),
        scratch_shapes=[
            pltpu.VMEM((X_SLOTS, tile, d), F32),
            pltpu.SemaphoreType.DMA((X_SLOTS,)),
            pltpu.VMEM((tile + 2 * HALO, d), BF16),
            pltpu.VMEM((2, 2, tile + 2 * HALO, FF_CHUNK), F32),
            pltpu.VMEM((tile, FF_DIM), BF16),
        ],
        compiler_params=_params("arbitrary"),
        name="conv_ffn",
    )(x2, x2, x2, mod, gain.reshape(1, d), w_up_all, w_conv, b_conv.reshape(1, 2 * FF_DIM), w_down_all,
      g_final.reshape(1, d))
    return out.reshape(b, s, d)


def _trunk(x, mods, norm_g, a_w_in, a_g_v, a_w_s, a_b_s, a_w_out, b_w_qkv, b_sinks, b_w_o, c_w_in, c_w_out,
           f_w_up, f_w_conv, f_b_conv, f_w_down, g_final):
    for i in range(DEPTH):
        mod = mods[i]
        kind, j = i % N_MIXERS, i // N_MIXERS
        if kind == 0:
            x = _gmlp_layer(x, mod, norm_g[i, 0], a_w_in[j], a_g_v[j], a_w_s[j], a_b_s[j], a_w_out[j])
        elif kind == 1:
            q, k, v = _qkv_proj(x, mod, norm_g[i, 0], b_w_qkv[j])
            x = _attn_layer(x, mod, q, k, v, b_sinks[j], b_w_o[j])
        else:
            x = _fourier_layer(x, mod, norm_g[i, 0], c_w_in[j], c_w_out[j])
        x = _ffn_layer(x, mod, norm_g[i, 1], f_w_up, f_w_conv[i], f_b_conv[i], f_w_down, g_final, layer=i,
                       final=(i == DEPTH - 1))
    return x


def kernel(x_prompt, x_sample, c_prompt, c_sample, w_ada, b_ada, norm_g, a_w_in, a_g_v, a_w_s, a_b_s, a_w_out,
           b_w_qkv, b_sinks, b_w_o, c_w_in, c_w_out, f_w_up, f_w_conv, f_b_conv, f_w_down, g_final):
    nb_p, nb_s = x_prompt.shape[0], x_sample.shape[0]
    assert nb_p + nb_s <= MOD_ROWS
    c_all = jnp.concatenate(
        [c_prompt, c_sample, jnp.zeros((MOD_ROWS - nb_p - nb_s, D_MODEL), F32)], axis=0)
    mods = _modulation(c_all, w_ada, b_ada).reshape(DEPTH, MOD_ROWS, 6, D_MODEL)
    weights = (norm_g, a_w_in, a_g_v, a_w_s, a_b_s, a_w_out, b_w_qkv, b_sinks, b_w_o, c_w_in, c_w_out,
               f_w_up.astype(BF16), f_w_conv, f_b_conv, f_w_down.astype(BF16), g_final)
    y_prompt = _trunk(x_prompt, mods[:, :nb_p], *weights)
    y_sample = _trunk(x_sample, mods[:, nb_p:nb_p + nb_s], *weights)
    return (y_prompt, y_sample)
```

```python
import functools

import numpy as np
import jax
import jax.numpy as jnp
from jax import lax
from jax.experimental import pallas as pl
from jax.experimental.pallas import tpu as pltpu

F32 = jnp.float32
BF16 = jnp.bfloat16

D_MODEL = 1024
DEPTH = 4
N_MIXERS = 3
EPS = 1e-6
NEG_INF = -1e30
CHUNK = 128
A_GROUPS = 8
A_HEAD = D_MODEL // A_GROUPS
B_HEADS = 16
B_KV_HEADS = 4
B_HEAD_DIM = 64
B_Q_PER_KV = B_HEADS // B_KV_HEADS
WINDOW = 128
BLOCK = 128
Q_WIDTH = B_HEADS * B_HEAD_DIM
KV_WIDTH = B_KV_HEADS * B_HEAD_DIM
C_GROUPS = 8
C_GROUP_DIM = D_MODEL // C_GROUPS
FF_DIM = 2816
FF_CHUNK = 256
HALO = 8

MOD_ROWS = 8
MOD_NT = 1536
VMEM_LIMIT = 56 * 1024 * 1024

TOKEN_TILE = 512


def _params(*sem):
    return pltpu.CompilerParams(dimension_semantics=sem, vmem_limit_bytes=VMEM_LIMIT)


def _norm_mod(x, gain, scale, shift):
    ms = jnp.mean(x * x, axis=-1, keepdims=True)
    return x * lax.rsqrt(ms + EPS) * (gain * (1.0 + scale)) + shift


def _regroup_rows(x, outer):
    rows, cols = x.shape
    return jnp.transpose(x.reshape(outer, rows // outer, cols), (1, 0, 2)).reshape(rows, cols)


def _mod_kernel(c_ref, w_ref, b_ref, o_ref):
    c = c_ref[...]
    cs = c * jax.nn.sigmoid(c)
    o_ref[0] = jnp.dot(cs, w_ref[0], preferred_element_type=F32) + b_ref[0]


def _modulation(c_all, w_ada, b_ada):
    n_out = w_ada.shape[-1]
    return pl.pallas_call(
        _mod_kernel,
        grid=(DEPTH, n_out // MOD_NT),
        in_specs=[
            pl.BlockSpec((MOD_ROWS, D_MODEL), lambda i, j: (0, 0)),
            pl.BlockSpec((1, D_MODEL, MOD_NT), lambda i, j: (i, 0, j)),
            pl.BlockSpec((1, 1, MOD_NT), lambda i, j: (i, 0, j)),
        ],
        out_specs=pl.BlockSpec((1, MOD_ROWS, MOD_NT), lambda i, j: (i, 0, j)),
        out_shape=jax.ShapeDtypeStruct((DEPTH, MOD_ROWS, n_out), F32),
        compiler_params=_params("parallel", "parallel"),
        name="modulation",
    )(c_all, w_ada, b_ada.reshape(DEPTH, 1, n_out))


def _gmlp_kernel(x_ref, mod_ref, g_ref, win_ref, gv_ref, ws_ref, bs_ref, wout_ref, o_ref, gated_ref, *, tile):
    mod = mod_ref[0]
    half = tile // 2
    n_sub = half // CHUNK
    pair = 2 * A_HEAD
    n_pairs = D_MODEL // pair
    inv_sqrt2 = np.float32(1.0 / np.sqrt(2.0))

    def gelu(t):
        return 0.5 * t * (1.0 + lax.erf(t * inv_sqrt2))

    halves = [slice(i * half, (i + 1) * half) for i in range(2)]
    hs = [_norm_mod(x_ref[0, r, :], g_ref[...], mod[1:2], mod[0:1]).astype(BF16) for r in halves]
    vs = [jnp.dot(h, win_ref[:, D_MODEL:], preferred_element_type=F32) for h in hs]
    for i, r in enumerate(halves):
        h = hs[i]
        v = gelu(vs[i])
        v = v * lax.rsqrt(jnp.mean(v * v, axis=-1, keepdims=True) + EPS) * gv_ref[...]
        vb = v.astype(BF16)

        def u_proj(j):
            return jnp.dot(h, win_ref[:, j * pair:(j + 1) * pair], preferred_element_type=F32)

        u_next = u_proj(0)
        for j in range(n_pairs):
            u = gelu(u_next)
            if j + 1 < n_pairs:
                u_next = u_proj(j + 1)
            for gi in range(2):
                g = 2 * j + gi
                cols = slice(g * A_HEAD, (g + 1) * A_HEAD)
                rhs = jnp.concatenate([vb[c * CHUNK:(c + 1) * CHUNK, cols] for c in range(n_sub)], axis=1)
                sv = jnp.dot(ws_ref[g], rhs, preferred_element_type=F32)
                for c in range(n_sub):
                    rows = slice(c * CHUNK, (c + 1) * CHUNK)
                    gated_ref[i, rows, cols] = (u[rows, gi * A_HEAD:(gi + 1) * A_HEAD]
                                                * (sv[:, c * CHUNK:(c + 1) * CHUNK] + bs_ref[g])).astype(BF16)
        m = jnp.dot(gated_ref[i], wout_ref[...], preferred_element_type=F32)
        o_ref[0, r, :] = x_ref[0, r, :] + mod[2:3] * m


def _gmlp_layer(x, mod, gain, w_in, g_v, w_s, b_s, w_out):
    b, s, d = x.shape
    tile = min(2 * TOKEN_TILE, s)
    bs_full = jnp.broadcast_to(b_s[:, :, None], (A_GROUPS, CHUNK, A_HEAD))
    const2 = lambda bi, i: (0, 0)
    const3 = lambda bi, i: (0, 0, 0)
    return pl.pallas_call(
        functools.partial(_gmlp_kernel, tile=tile),
        grid=(b, s // tile),
        in_specs=[
            pl.BlockSpec((1, tile, d), lambda bi, i: (bi, i, 0)),
            pl.BlockSpec((1, 6, d), lambda bi, i: (bi, 0, 0)),
            pl.BlockSpec((1, d), const2),
            pl.BlockSpec((d, 2 * d), const2),
            pl.BlockSpec((1, d), const2),
            pl.BlockSpec((A_GROUPS, CHUNK, CHUNK), const3),
            pl.BlockSpec((A_GROUPS, CHUNK, A_HEAD), const3),
            pl.BlockSpec((d, d), const2),
        ],
        out_specs=pl.BlockSpec((1, tile, d), lambda bi, i: (bi, i, 0)),
        out_shape=jax.ShapeDtypeStruct(x.shape, F32),
        scratch_shapes=[pltpu.VMEM((2, tile // 2, d), BF16)],
        compiler_params=_params("parallel", "parallel"),
        name="gmlp_mixer",
    )(x, mod, gain.reshape(1, d), w_in.astype(BF16), g_v.reshape(1, d), w_s.astype(BF16), bs_full,
      w_out.astype(BF16))


_PAIRS = B_KV_HEADS // 2
_HEAD_ORDER = [((2 * (t // B_Q_PER_KV) + half) * B_Q_PER_KV + t % B_Q_PER_KV)
               for t in range(B_HEADS // 2) for half in range(2)]
_HEAD_COLS = np.concatenate([np.arange(h * B_HEAD_DIM, (h + 1) * B_HEAD_DIM) for h in _HEAD_ORDER])
GROUP_ROWS = B_Q_PER_KV * BLOCK
BAND = 3 * BLOCK
PAIR_WIDTH = 2 * B_HEAD_DIM
LOG2E = float(np.log2(np.e))
_NT = (((1,), (1,)), ((), ()))


def _qkv_kernel(x_ref, mod_ref, g_ref, wqk_ref, wvt_ref, q_ref, k_ref, vt_ref, *, tile):
    mod = mod_ref[0]
    h = _norm_mod(x_ref[0], g_ref[...], mod[1:2], mod[0:1]).astype(BF16)
    qk = jnp.dot(h, wqk_ref[...], preferred_element_type=F32)
    for blk in range(tile // BLOCK):
        rows = slice(blk * BLOCK, (blk + 1) * BLOCK)
        for t in range(Q_WIDTH // PAIR_WIDTH):
            q_ref[0, blk, t * BLOCK:(t + 1) * BLOCK, :] = (
                qk[rows, t * PAIR_WIDTH:(t + 1) * PAIR_WIDTH] * (B_HEAD_DIM ** -0.5 * LOG2E)).astype(BF16)
    k_ref[0] = qk[:, Q_WIDTH:].astype(BF16)
    vt_ref[0] = lax.dot_general(wvt_ref[...], h, _NT, preferred_element_type=F32).astype(BF16)


def _qkv_proj(x, mod, gain, w_qkv):
    b, s, d = x.shape
    tile = min(2 * TOKEN_TILE, s)
    tok = lambda bi, i: (bi, i, 0)
    const2 = lambda bi, i: (0, 0)
    w_qk = jnp.concatenate([w_qkv[:, _HEAD_COLS], w_qkv[:, Q_WIDTH:Q_WIDTH + KV_WIDTH]], axis=1).astype(BF16)
    w_vt = w_qkv[:, Q_WIDTH + KV_WIDTH:].T.astype(BF16)
    return pl.pallas_call(
        functools.partial(_qkv_kernel, tile=tile),
        grid=(b, s // tile),
        in_specs=[
            pl.BlockSpec((1, tile, d), tok),
            pl.BlockSpec((1, 6, d), lambda bi, i: (bi, 0, 0)),
            pl.BlockSpec((1, d), const2),
            pl.BlockSpec((d, Q_WIDTH + KV_WIDTH), const2),
            pl.BlockSpec((KV_WIDTH, d), const2),
        ],
        out_specs=[
            pl.BlockSpec((1, tile // BLOCK, Q_WIDTH, PAIR_WIDTH), lambda bi, i: (bi, i, 0, 0)),
            pl.BlockSpec((1, tile, KV_WIDTH), tok),
            pl.BlockSpec((1, KV_WIDTH, tile), lambda bi, i: (bi, 0, i)),
        ],
        out_shape=[
            jax.ShapeDtypeStruct((b, s // BLOCK, Q_WIDTH, PAIR_WIDTH), BF16),
            jax.ShapeDtypeStruct((b, s, KV_WIDTH), BF16),
            jax.ShapeDtypeStruct((b, KV_WIDTH, s), BF16),
        ],
        compiler_params=_params("parallel", "parallel"),
        name="qkv_proj",
    )(x, mod, gain.reshape(1, d), w_qk, w_vt)


def _attn_bias_table():
    qq = np.arange(BLOCK)[None, :]
    kk = np.arange(BAND)[:, None]
    dist = np.abs(qq + BLOCK - kk)
    slopes = np.exp2(np.float32(-8.0) * np.arange(1, B_HEADS + 1, dtype=np.float32) / np.float32(B_HEADS))
    tbl = np.empty((3, _PAIRS, 2 * BAND, GROUP_ROWS), np.float32)
    for case in range(3):
        valid = dist <= WINDOW
        if case == 0:
            valid = valid & (kk >= BLOCK)
        if case == 2:
            valid = valid & (kk < 2 * BLOCK)
        for p in range(_PAIRS):
            for c in range(B_Q_PER_KV):
                for half in range(2):
                    hd = (2 * p + half) * B_Q_PER_KV + c
                    alibi = (-slopes[hd].astype(np.float64) * dist * LOG2E).astype(np.float32)
                    tbl[case, p, half * BAND:(half + 1) * BAND, c * BLOCK:(c + 1) * BLOCK] = np.where(
                        valid, alibi, np.float32(NEG_INF))
    return tbl


def _attn_kernel(x_ref, mod_ref, q_ref, kp_ref, k_ref, kn_ref, vp_ref, v_ref, vn_ref, bias_ref, sink_ref, wo_ref,
                 o_ref, kext_ref, vext_ref, ocat_ref, *, tile, nblk):
    i = pl.program_id(1)
    bpt = tile // BLOCK
    kext_ref[0:BLOCK] = kp_ref[0]
    kext_ref[BLOCK:BLOCK + tile] = k_ref[0]
    kext_ref[BLOCK + tile:] = kn_ref[0]
    vext_ref[0] = vp_ref[0]
    for j in range(bpt):
        vext_ref[1 + j] = v_ref[0, :, j * BLOCK:(j + 1) * BLOCK]
    vext_ref[1 + bpt] = vn_ref[0]
    low_lane = lax.broadcasted_iota(jnp.int32, (BAND, PAIR_WIDTH), 1) < B_HEAD_DIM
    low_row = lax.broadcasted_iota(jnp.int32, (PAIR_WIDTH, BAND), 0) < B_HEAD_DIM
    low_out = lax.broadcasted_iota(jnp.int32, (PAIR_WIDTH, GROUP_ROWS), 0) < B_HEAD_DIM
    ones_row = lax.broadcasted_iota(jnp.int32, (2 * HALO, 2 * BAND), 0) < HALO
    first_half = lax.broadcasted_iota(jnp.int32, (2 * HALO, 2 * BAND), 1) < BAND
    ones = jnp.where(ones_row == first_half, 1.0, 0.0).astype(BF16)

    def scores(qb, p):
        rows = slice(p * PAIR_WIDTH, (p + 1) * PAIR_WIDTH)
        n = i * bpt + qb
        case = jnp.where(n == 0, 0, jnp.where(n == nblk - 1, 2, 1))
        kb = kext_ref[qb * BLOCK:qb * BLOCK + BAND, rows]
        zk = jnp.zeros_like(kb)
        kz = jnp.concatenate([jnp.where(low_lane, kb, zk), jnp.where(low_lane, zk, kb)], axis=0)
        q4 = q_ref[0, qb, p * GROUP_ROWS:(p + 1) * GROUP_ROWS, :]
        return lax.dot_general(kz, q4, _NT, preferred_element_type=F32) + bias_ref[case, p]

    def attend(qb, p, sc):
        rows = slice(p * PAIR_WIDTH, (p + 1) * PAIR_WIDTH)
        vt = jnp.concatenate([vext_ref[qb, rows, :], vext_ref[qb + 1, rows, :], vext_ref[qb + 2, rows, :]], axis=1)
        zv = jnp.zeros_like(vt)
        vz = jnp.concatenate([
            jnp.concatenate([jnp.where(low_row, vt, zv), jnp.where(low_row, zv, vt)], axis=1), ones], axis=0)
        probs = []
        shift = []
        for half in range(2):
            sh = sc[half * BAND:(half + 1) * BAND]
            sink = sink_ref[p, half]
            mx = jnp.maximum(jnp.max(sh, axis=0, keepdims=True), sink)
            probs.append(jnp.exp2(sh - mx).astype(BF16))
            shift.append(jnp.exp2(sink - mx))
        pv = jnp.dot(vz, jnp.concatenate(probs, axis=0), preferred_element_type=F32)
        inv_a = 1.0 / (pv[PAIR_WIDTH:PAIR_WIDTH + 1] + shift[0])
        inv_b = 1.0 / (pv[PAIR_WIDTH + HALO:PAIR_WIDTH + HALO + 1] + shift[1])
        on = pv[:PAIR_WIDTH] * jnp.where(low_out, inv_a, inv_b)
        for c in range(B_Q_PER_KV):
            t = p * B_Q_PER_KV + c
            ocat_ref[qb * BLOCK:(qb + 1) * BLOCK, t * PAIR_WIDTH:(t + 1) * PAIR_WIDTH] = on[:, c * BLOCK:(c + 1) * BLOCK].T

    units = [(qb, p) for qb in range(bpt) for p in range(_PAIRS)]
    sc = scores(*units[0])
    for u, unit in enumerate(units):
        sc_next = scores(*units[u + 1]) if u + 1 < len(units) else None
        attend(*unit, sc)
        sc = sc_next
    m = jnp.dot(ocat_ref[...].astype(BF16), wo_ref[...], preferred_element_type=F32)
    o_ref[0] = x_ref[0] + mod_ref[0][2:3] * m


def _attn_layer(x, mod, q, k, vt, sinks, w_o):
    b, s, d = x.shape
    tile = min(2 * TOKEN_TILE, s)
    bpt = tile // BLOCK
    nblk = s // BLOCK
    assert nblk >= 2
    tok = lambda bi, i: (bi, i, 0)
    prev = lambda bi, i: jnp.maximum(i * bpt - 1, 0)
    nxt = lambda bi, i: jnp.minimum((i + 1) * bpt, nblk - 1)
    bias = jnp.asarray(_attn_bias_table())
    head = np.array([[[(2 * p + half) * B_Q_PER_KV + c for c in range(B_Q_PER_KV)] for half in range(2)]
                     for p in range(_PAIRS)])
    sink_tbl = jnp.repeat(sinks.astype(F32)[head] * LOG2E, BLOCK, axis=-1)[:, :, None, :]
    return pl.pallas_call(
        functools.partial(_attn_kernel, tile=tile, nblk=nblk),
        grid=(b, s // tile),
        in_specs=[
            pl.BlockSpec((1, tile, d), tok),
            pl.BlockSpec((1, 6, d), lambda bi, i: (bi, 0, 0)),
            pl.BlockSpec((1, bpt, Q_WIDTH, PAIR_WIDTH), lambda bi, i: (bi, i, 0, 0)),
            pl.BlockSpec((1, BLOCK, KV_WIDTH), lambda bi, i: (bi, prev(bi, i), 0)),
            pl.BlockSpec((1, tile, KV_WIDTH), tok),
            pl.BlockSpec((1, BLOCK, KV_WIDTH), lambda bi, i: (bi, nxt(bi, i), 0)),
            pl.BlockSpec((1, KV_WIDTH, BLOCK), lambda bi, i: (bi, 0, prev(bi, i))),
            pl.BlockSpec((1, KV_WIDTH, tile), lambda bi, i: (bi, 0, i)),
            pl.BlockSpec((1, KV_WIDTH, BLOCK), lambda bi, i: (bi, 0, nxt(bi, i))),
            pl.BlockSpec(bias.shape, lambda bi, i: (0, 0, 0, 0)),
            pl.BlockSpec(sink_tbl.shape, lambda bi, i: (0, 0, 0, 0)),
            pl.BlockSpec((Q_WIDTH, d), lambda bi, i: (0, 0)),
        ],
        out_specs=pl.BlockSpec((1, tile, d), tok),
        out_shape=jax.ShapeDtypeStruct(x.shape, F32),
        scratch_shapes=[
            pltpu.VMEM((tile + 2 * BLOCK, KV_WIDTH), BF16),
            pltpu.VMEM((bpt + 2, KV_WIDTH, BLOCK), BF16),
            pltpu.VMEM((tile, Q_WIDTH), F32),
        ],
        compiler_params=_params("parallel", "parallel"),
        name="window_attention",
    )(x, mod, q, k, k, k, vt, vt, vt, bias, sink_tbl, w_o[_HEAD_COLS].astype(BF16))


def _dft_tables(seq):
    n1 = BLOCK
    n2 = seq // n1
    k1 = np.arange(n1, dtype=np.int64)
    s1 = np.arange(n1, dtype=np.int64)
    s2 = np.arange(n2, dtype=np.int64)
    pos = s1[None, None, :] * n2 + s2[:, None, None]
    ang = 2.0 * np.pi * ((k1[None, :, None] * pos) % seq).astype(np.float64) / seq
    stage1 = np.concatenate([np.cos(ang), -np.sin(ang)], axis=1).astype(np.float32)
    k2 = np.arange(n2, dtype=np.int64)
    ang2 = 2.0 * np.pi * ((k2[:, None] * s2[None, :]) % n2).astype(np.float64) / n2
    c2, sn2 = np.cos(ang2), np.sin(ang2)
    stage2 = np.block([[c2, sn2], [-sn2, c2]]).astype(np.float32)
    stage2 = stage2.reshape(2 * n2, 2, n2).transpose(0, 2, 1).reshape(2 * n2, 2 * n2)
    c = np.arange(C_GROUP_DIM, dtype=np.int64)
    angc = 2.0 * np.pi * ((c[:, None] * c[None, :]) % C_GROUP_DIM).astype(np.float64) / C_GROUP_DIM
    chan = (np.concatenate([np.cos(angc), np.sin(angc)], axis=0) / np.sqrt(float(seq) * C_GROUP_DIM)).astype(np.float32)
    return n1, n2, stage1, stage2, chan


def _fourier1_kernel(x_ref, mod_ref, g_ref, win_ref, dft_ref, mid_ref, *, n1, grp):
    mod = mod_ref[0]
    x = x_ref[0].reshape(n1 * grp, D_MODEL)
    h = _norm_mod(x, g_ref[...], mod[1:2], mod[0:1]).astype(BF16)
    z = jnp.dot(h, win_ref[...], preferred_element_type=F32)
    z = _regroup_rows(z, n1).astype(BF16)
    y = jnp.concatenate([jnp.dot(dft_ref[g], z[g * n1:(g + 1) * n1], preferred_element_type=F32)
                         for g in range(grp)], axis=0)
    mid_ref[0] = _regroup_rows(y, 2 * grp).astype(BF16).reshape(n1, 2 * grp, D_MODEL)


def _fourier2_kernel(x_ref, mod_ref, mid_ref, dft_ref, chan_ref, wout_ref, o_ref, re_ref, im_ref, f_ref, *, n2, grp):
    for kl in range(grp):
        spec = jnp.dot(dft_ref[...], mid_ref[0, kl], preferred_element_type=F32)
        re_ref[kl * n2:(kl + 1) * n2] = spec[:n2].astype(BF16)
        im_ref[kl * n2:(kl + 1) * n2] = spec[n2:].astype(BF16)
    for cg in range(C_GROUPS):
        cols = slice(cg * C_GROUP_DIM, (cg + 1) * C_GROUP_DIM)
        lhs = jnp.concatenate([re_ref[:, cols], im_ref[:, cols]], axis=1)
        f_ref[:, cols] = jnp.dot(lhs, chan_ref[...], preferred_element_type=F32).astype(BF16)
    m = jnp.dot(f_ref[...], wout_ref[...], preferred_element_type=F32)
    m = _regroup_rows(m, grp)
    out = x_ref[0].reshape(n2 * grp, D_MODEL) + mod_ref[0][2:3] * m
    o_ref[0] = out.reshape(n2, grp, D_MODEL)


def _fourier_layer(x, mod, gain, w_in, w_out):
    b, s, d = x.shape
    n1, n2, stage1, stage2, chan = _dft_tables(s)
    grp = HALO
    const2 = lambda bi, j: (0, 0)
    mid = pl.pallas_call(
        functools.partial(_fourier1_kernel, n1=n1, grp=grp),
        grid=(b, n2 // grp),
        in_specs=[
            pl.BlockSpec((1, n1, grp, d), lambda bi, j: (bi, 0, j, 0)),
            pl.BlockSpec((1, 6, d), lambda bi, j: (bi, 0, 0)),
            pl.BlockSpec((1, d), const2),
            pl.BlockSpec((d, d), const2),
            pl.BlockSpec((grp, 2 * n1, n1), lambda bi, j: (j, 0, 0)),
        ],
        out_specs=pl.BlockSpec((1, n1, 2 * grp, d), lambda bi, j: (bi, 0, j, 0)),
        out_shape=jax.ShapeDtypeStruct((b, n1, 2 * n2, d), BF16),
        compiler_params=_params("parallel", "parallel"),
        name="fourier_stage1",
    )(x.reshape(b, n1, n2, d), mod, gain.reshape(1, d), w_in.astype(BF16), jnp.asarray(stage1).astype(BF16))
    grp = max(HALO, TOKEN_TILE * 2 // n2)
    out = pl.pallas_call(
        functools.partial(_fourier2_kernel, n2=n2, grp=grp),
        grid=(b, n1 // grp),
        in_specs=[
            pl.BlockSpec((1, n2, grp, d), lambda bi, j: (bi, 0, j, 0)),
            pl.BlockSpec((1, 6, d), lambda bi, j: (bi, 0, 0)),
            pl.BlockSpec((1, grp, 2 * n2, d), lambda bi, j: (bi, j, 0, 0)),
            pl.BlockSpec((2 * n2, 2 * n2), const2),
            pl.BlockSpec((2 * C_GROUP_DIM, C_GROUP_DIM), const2),
            pl.BlockSpec((d, d), const2),
        ],
        out_specs=pl.BlockSpec((1, n2, grp, d), lambda bi, j: (bi, 0, j, 0)),
        out_shape=jax.ShapeDtypeStruct((b, n2, n1, d), F32),
        scratch_shapes=[pltpu.VMEM((grp * n2, d), BF16)] * 3,
        compiler_params=_params("parallel", "parallel"),
        name="fourier_stage2",
    )(x.reshape(b, n2, n1, d), mod, mid, jnp.asarray(stage2).astype(BF16),
      jnp.asarray(chan).astype(BF16), w_out.astype(BF16))
    return out.reshape(b, s, d)


def _ffn_kernel(xp_ref, x_ref, xn_ref, mod_ref, g_ref, wup_ref, wc_ref, bc_ref, wdn_ref, gf_ref, o_ref,
                h_ref, up_ref, act_ref, *, tile, final):
    i = pl.program_id(1)
    last = pl.num_programs(1) - 1
    mod = mod_ref[0]
    gain = g_ref[...]
    x = _regroup_rows(x_ref[0], HALO)
    h_ref[0:tile] = _norm_mod(x, gain, mod[4:5], mod[3:4]).astype(BF16)
    h_next = _norm_mod(xn_ref[0], gain, mod[4:5], mod[3:4]) * (i < last).astype(F32)
    h_prev = _norm_mod(xp_ref[0], gain, mod[4:5], mod[3:4]) * (i > 0).astype(F32)
    h_ref[tile:] = jnp.concatenate([h_next, h_prev], axis=0).astype(BF16)
    h = h_ref[...]
    n_chunks = FF_DIM // FF_CHUNK
    sub = lax.broadcasted_iota(jnp.int32, (HALO, FF_CHUNK), 0)

    def up_proj(j):
        for half in range(2):
            c0 = half * FF_DIM + j * FF_CHUNK
            up = jnp.dot(h, wup_ref[:, c0:c0 + FF_CHUNK], preferred_element_type=F32)
            dst = up_ref.at[j % 2, half]
            dst[HALO:HALO + tile] = up[:tile]
            dst[0:HALO] = jnp.where(sub == 0, pltpu.roll(up[tile + HALO:], 1, 0),
                                    pltpu.roll(up[tile - HALO:tile], 1, 0))
            dst[HALO + tile:] = jnp.where(sub == HALO - 1, pltpu.roll(up[tile:tile + HALO], HALO - 1, 0),
                                          pltpu.roll(up[0:HALO], HALO - 1, 0))

    def conv(j, half):
        cols = slice(half * FF_DIM + j * FF_CHUNK, half * FF_DIM + (j + 1) * FF_CHUNK)
        w = wc_ref[:, cols]
        src = up_ref.at[j % 2, half]
        return (src[0:tile] * w[0:1] + src[HALO:HALO + tile] * w[1:2] + src[2 * HALO:2 * HALO + tile] * w[2:3]
                + bc_ref[:, cols])

    up_proj(0)
    for j in range(n_chunks):
        if j + 1 < n_chunks:
            up_proj(j + 1)
        a = conv(j, 0)
        g = conv(j, 1)
        act_ref[:, j * FF_CHUNK:(j + 1) * FF_CHUNK] = (a * (g * jax.nn.sigmoid(g))).astype(BF16)
    y = x + mod[5:6] * jnp.dot(act_ref[...], wdn_ref[...], preferred_element_type=F32)
    if final:
        y = y * lax.rsqrt(jnp.mean(y * y, axis=-1, keepdims=True) + EPS) * gf_ref[...]
    o_ref[0] = _regroup_rows(y, tile // HALO)


def _ffn_layer(x, mod, gain, w_up_all, w_conv, b_conv, w_down_all, g_final, layer, final):
    b, s, d = x.shape
    tile = min(2 * TOKEN_TILE, s)
    hpt = tile // HALO
    nh = s // HALO
    tok = lambda bi, i: (bi, i, 0)
    const2 = lambda bi, i: (0, 0)
    return pl.pallas_call(
        functools.partial(_ffn_kernel, tile=tile, final=final),
        grid=(b, s // tile),
        in_specs=[
            pl.BlockSpec((1, HALO, d), lambda bi, i: (bi, jnp.maximum(i * hpt - 1, 0), 0)),
            pl.BlockSpec((1, tile, d), tok),
            pl.BlockSpec((1, HALO, d), lambda bi, i: (bi, jnp.minimum((i + 1) * hpt, nh - 1), 0)),
            pl.BlockSpec((1, 6, d), lambda bi, i: (bi, 0, 0)),
            pl.BlockSpec((1, d), const2),
            pl.BlockSpec((None, d, 2 * FF_DIM), lambda bi, i: (layer, 0, 0)),
            pl.BlockSpec((3, 2 * FF_DIM), const2),
            pl.BlockSpec((1, 2 * FF_DIM), const2),
            pl.BlockSpec((None, FF_DIM, d), lambda bi, i: (layer, 0, 0)),
            pl.BlockSpec((1, d), const2),
        ],
        out_specs=pl.BlockSpec((1, tile, d), tok),
        out_shape=jax.ShapeDtypeStruct(x.shape, F32),
        scratch_shapes=[
            pltpu.VMEM((tile + 2 * HALO, d), BF16),
            pltpu.VMEM((2, 2, tile + 2 * HALO, FF_CHUNK), F32),
            pltpu.VMEM((tile, FF_DIM), BF16),
        ],
        compiler_params=_params("parallel", "parallel"),
        name="conv_ffn",
    )(x, x, x, mod, gain.reshape(1, d), w_up_all, w_conv, b_conv.reshape(1, 2 * FF_DIM), w_down_all,
      g_final.reshape(1, d))


def _trunk(x, mods, norm_g, a_w_in, a_g_v, a_w_s, a_b_s, a_w_out, b_w_qkv, b_sinks, b_w_o, c_w_in, c_w_out,
           f_w_up, f_w_conv, f_b_conv, f_w_down, g_final):
    for i in range(DEPTH):
        mod = mods[i]
        kind, j = i % N_MIXERS, i // N_MIXERS
        if kind == 0:
            x = _gmlp_layer(x, mod, norm_g[i, 0], a_w_in[j], a_g_v[j], a_w_s[j], a_b_s[j], a_w_out[j])
        elif kind == 1:
            q, k, v = _qkv_proj(x, mod, norm_g[i, 0], b_w_qkv[j])
            x = _attn_layer(x, mod, q, k, v, b_sinks[j], b_w_o[j])
        else:
            x = _fourier_layer(x, mod, norm_g[i, 0], c_w_in[j], c_w_out[j])
        x = _ffn_layer(x, mod, norm_g[i, 1], f_w_up, f_w_conv[i], f_b_conv[i], f_w_down, g_final, layer=i,
                       final=(i == DEPTH - 1))
    return x


def kernel(x_prompt, x_sample, c_prompt, c_sample, w_ada, b_ada, norm_g, a_w_in, a_g_v, a_w_s, a_b_s, a_w_out,
           b_w_qkv, b_sinks, b_w_o, c_w_in, c_w_out, f_w_up, f_w_conv, f_b_conv, f_w_down, g_final):
    nb_p, nb_s = x_prompt.shape[0], x_sample.shape[0]
    assert nb_p + nb_s <= MOD_ROWS
    c_all = jnp.concatenate(
        [c_prompt, c_sample, jnp.zeros((MOD_ROWS - nb_p - nb_s, D_MODEL), F32)], axis=0)
    mods = _modulation(c_all, w_ada, b_ada).reshape(DEPTH, MOD_ROWS, 6, D_MODEL)
    weights = (norm_g, a_w_in, a_g_v, a_w_s, a_b_s, a_w_out, b_w_qkv, b_sinks, b_w_o, c_w_in, c_w_out,
               f_w_up.astype(BF16), f_w_conv, f_b_conv, f_w_down.astype(BF16), g_final)
    y_prompt = _trunk(x_prompt, mods[:, :nb_p], *weights)
    y_sample = _trunk(x_sample, mods[:, nb_p:nb_p + nb_s], *weights)
    return (y_prompt, y_sample)
```
